```python
import math
import jax, jax.numpy as jnp
from jax import lax
import numpy as np

D_MODEL = 1024
BATCH = 8
SEQ = 4096
DEPTH = 1
DEC_BATCH = 128
DEC_SEQ = 8
PAST_LEN = 8192
PAGE_SIZE = 128

ATTN_HEADS = 8
HEAD_DIM = 64
ATTN_WIDTH = ATTN_HEADS * HEAD_DIM
MOBA_BLOCK = 256
MOBA_TOPK = 3
Q_BLOCK = 64
SSM_WIDTH = D_MODEL - ATTN_WIDTH
SSM_GROUP = 16
SSM_GROUPS = SSM_WIDTH // SSM_GROUP
SSM_STATE = 64
MIX_WIDTH = ATTN_WIDTH + SSM_WIDTH
IN_WIDTH = 3 * ATTN_WIDTH + SSM_WIDTH
MEM_TOKENS = 256
MEM_HEADS = 4
MEM_HEAD_DIM = 128
MEM_WIDTH = MEM_HEADS * MEM_HEAD_DIM
D_FF = 4 * D_MODEL
RMS_EPS = 1e-6
DT_MIN = 1e-3
DT_MAX = 1e-1

kernel_name = "hymba_moba_s5_memory_decoder_step"


def rmsnorm(x, g):
    xf = x.astype(jnp.float32)
    y = xf * lax.rsqrt(jnp.mean(xf * xf, axis=-1, keepdims=True) + RMS_EPS)
    return (y * g.astype(jnp.float32)).astype(x.dtype)


def alibi_slopes():
    return jnp.asarray([2.0 ** (-8.0 * (h + 1) / ATTN_HEADS) for h in range(ATTN_HEADS)], dtype=jnp.float32)


def moba_query_block(q, pos, kb, vb, kmean, slopes):
    nq = q.shape[0]
    nb = kb.shape[1]
    own = pos // MOBA_BLOCK
    gate = jnp.einsum("qhd,hnd->qhn", q.astype(jnp.float32), kmean)
    fully_past = jnp.arange(nb)[None, None, :] < own[:, None, None]
    gate = jnp.where(fully_past, gate, -jnp.inf)
    top_val, top_idx = lax.top_k(gate, min(MOBA_TOPK, nb))
    idx = jnp.concatenate([top_idx, jnp.broadcast_to(own[:, None, None], (nq, ATTN_HEADS, 1))], axis=-1)
    blk_ok = jnp.concatenate([jnp.isfinite(top_val), jnp.ones((nq, ATTN_HEADS, 1), dtype=bool)], axis=-1)
    h_ix = jnp.arange(ATTN_HEADS)[None, :, None]
    kg = kb[h_ix, idx]
    vg = vb[h_ix, idx]
    kpos = idx[..., None] * MOBA_BLOCK + jnp.arange(MOBA_BLOCK)
    dist = (pos[:, None, None, None] - kpos).astype(jnp.float32)
    s = jnp.einsum("qhd,qhsjd->qhsj", q, kg).astype(jnp.float32) * (HEAD_DIM ** -0.5)
    s = s - slopes[None, :, None, None] * dist
    ok = blk_ok[..., None] & (dist >= 0)
    s = jnp.where(ok, s, -jnp.inf).reshape(nq, ATTN_HEADS, -1)
    p = jax.nn.softmax(s, axis=-1).astype(vg.dtype)
    return jnp.einsum("qhn,qhnd->qhd", p, vg.reshape(nq, ATTN_HEADS, -1, HEAD_DIM))


def moba_sequence(q, k, v, pos0, q_block, slopes):
    t_len = q.shape[0]
    k_len = k.shape[0]
    nb = -(-k_len // MOBA_BLOCK)
    pad = nb * MOBA_BLOCK - k_len

    def to_blocks(a):
        a = jnp.pad(a, ((0, pad), (0, 0), (0, 0)))
        return a.reshape(nb, MOBA_BLOCK, ATTN_HEADS, HEAD_DIM).transpose(2, 0, 1, 3)

    kb, vb = to_blocks(k), to_blocks(v)
    kmean = jnp.mean(kb.astype(jnp.float32), axis=2)
    n_qb = t_len // q_block
    qs = q.reshape(n_qb, q_block, ATTN_HEADS, HEAD_DIM)
    ps = (pos0 + jnp.arange(t_len, dtype=jnp.int32)).reshape(n_qb, q_block)
    out = lax.map(lambda a: moba_query_block(a[0], a[1], kb, vb, kmean, slopes), (qs, ps))
    return out.reshape(t_len, ATTN_HEADS, HEAD_DIM)


def moba_prompt(q, k, v, slopes):
    q_block = min(Q_BLOCK, q.shape[1])
    return lax.map(lambda a: moba_sequence(a[0], a[1], a[2], 0, q_block, slopes), (q, k, v))


def moba_paged(q, k_new, v_new, cache_k, cache_v, page_table, layer, slopes):
    def one(a):
        qs, kn, vn, pt = a
        kp = cache_k[layer, pt].reshape(-1, ATTN_HEADS, HEAD_DIM)
        vp = cache_v[layer, pt].reshape(-1, ATTN_HEADS, HEAD_DIM)
        k_all = jnp.concatenate([kp, kn.astype(kp.dtype)], axis=0)
        v_all = jnp.concatenate([vp, vn.astype(vp.dtype)], axis=0)
        return moba_sequence(qs.astype(kp.dtype), k_all, v_all, kp.shape[0], qs.shape[0], slopes)
    return lax.map(one, (q, k_new, v_new, page_table))


def s5_discretize(a_re, a_im, log_dt, b_re, b_im):
    are = a_re.astype(jnp.float32)
    aim = a_im.astype(jnp.float32)
    dt = jnp.exp(log_dt.astype(jnp.float32))[:, None]
    mag = jnp.exp(are * dt)
    ang = aim * dt
    lb_re = mag * jnp.cos(ang)
    lb_im = mag * jnp.sin(ang)
    e_re = lb_re - 1.0
    e_im = lb_im
    den = are * are + aim * aim
    f_re = (e_re * are + e_im * aim) / den
    f_im = (e_im * are - e_re * aim) / den
    br = b_re.astype(jnp.float32)
    bi = b_im.astype(jnp.float32)
    bb_re = f_re[..., None] * br - f_im[..., None] * bi
    bb_im = f_re[..., None] * bi + f_im[..., None] * br
    return lb_re, lb_im, bb_re, bb_im


def complex_affine_combine(first, second):
    a1r, a1i, b1r, b1i = first
    a2r, a2i, b2r, b2i = second
    return (a2r * a1r - a2i * a1i,
            a2r * a1i + a2i * a1r,
            a2r * b1r - a2i * b1i + b2r,
            a2r * b1i + a2i * b1r + b2i)


def s5_branch(u, h0_re, h0_im, a_re, a_im, log_dt, b_re, b_im, c_re, c_im, d, w_glu, b_glu):
    lb_re, lb_im, bb_re, bb_im = s5_discretize(a_re, a_im, log_dt, b_re, b_im)
    cr = c_re.astype(jnp.float32)
    ci = c_im.astype(jnp.float32)
    d_g = d.astype(jnp.float32).reshape(SSM_GROUPS, SSM_GROUP)

    def one(args):
        u_s, hr0, hi0 = args
        ug = u_s.astype(jnp.float32).reshape(-1, SSM_GROUPS, SSM_GROUP)
        bu_re = jnp.einsum("tgc,gpc->tgp", ug, bb_re)
        bu_im = jnp.einsum("tgc,gpc->tgp", ug, bb_im)
        bu_re = bu_re.at[0].add(lb_re * hr0 - lb_im * hi0)
        bu_im = bu_im.at[0].add(lb_re * hi0 + lb_im * hr0)
        a_r = jnp.broadcast_to(lb_re, bu_re.shape)
        a_i = jnp.broadcast_to(lb_im, bu_im.shape)
        _, _, hr, hi = lax.associative_scan(complex_affine_combine, (a_r, a_i, bu_re, bu_im), axis=0)
        y = jnp.einsum("tgp,gcp->tgc", hr, cr) - jnp.einsum("tgp,gcp->tgc", hi, ci) + d_g * ug
        return y.reshape(-1, SSM_WIDTH), hr[-1], hi[-1]

    y, hr_t, hi_t = lax.map(one, (u, h0_re.astype(jnp.float32), h0_im.astype(jnp.float32)))
    y = y.astype(u.dtype)
    y = y * jax.nn.sigmoid(y @ w_glu + b_glu)
    return y, hr_t, hi_t


def memory_kv(mem, g_mem, w_ck, w_cv):
    b, m, _ = mem.shape
    hm = rmsnorm(mem, g_mem)
    mk = (hm @ w_ck).reshape(b, m, MEM_HEADS, MEM_HEAD_DIM)
    mv = (hm @ w_cv).reshape(b, m, MEM_HEADS, MEM_HEAD_DIM)
    return mk, mv


def memory_attention(h, mk, mv, w_cq, w_co):
    b, t, _ = h.shape
    q = (h @ w_cq).reshape(b, t, MEM_HEADS, MEM_HEAD_DIM)
    s = jnp.einsum("bthd,bshd->bhts", q, mk.astype(q.dtype)).astype(jnp.float32) * (MEM_HEAD_DIM ** -0.5)
    p = jax.nn.softmax(s, axis=-1).astype(h.dtype)
    o = jnp.einsum("bhts,bshd->bthd", p, mv.astype(h.dtype)).reshape(b, t, MEM_WIDTH)
    return o @ w_co


def decoder_layer(x, attend, h0_re, h0_im, mem_k, mem_v, lw):
    (g_mix, w_in, a_re, a_im, log_dt, b_re, b_im, c_re, c_im, d, w_glu, b_glu,
     w_out, g_cross, w_cq, w_co, g_mlp, w_ff1, w_ff2) = lw
    b, t, _ = x.shape
    z = rmsnorm(x, g_mix) @ w_in
    q, k, v, u = jnp.split(z, [ATTN_WIDTH, 2 * ATTN_WIDTH, 3 * ATTN_WIDTH], axis=-1)
    shp = (b, t, ATTN_HEADS, HEAD_DIM)
    q, k, v = q.reshape(shp), k.reshape(shp), v.reshape(shp)
    attn = attend(q, k, v).reshape(b, t, ATTN_WIDTH).astype(x.dtype)
    ssm, hr, hi = s5_branch(u, h0_re, h0_im, a_re, a_im, log_dt, b_re, b_im, c_re, c_im, d, w_glu, b_glu)
    x = x + jnp.concatenate([attn, ssm], axis=-1) @ w_out
    x = x + memory_attention(rmsnorm(x, g_cross), mem_k, mem_v, w_cq, w_co)
    hid = jnp.square(jax.nn.relu(rmsnorm(x, g_mlp) @ w_ff1))
    x = x + hid @ w_ff2
    return x, k, v, hr, hi


def setup_inputs(seed: int = 0) -> dict:
    key = jax.random.key(seed)
    ks = iter(jax.random.split(key, 48))

    def nrm(shape, scale):
        return jax.random.normal(next(ks), shape, jnp.float32) * scale

    L = DEPTH
    n_pages = PAST_LEN // PAGE_SIZE
    n_used = DEC_BATCH * n_pages
    n_phys = -(-5 * n_used // 4)
    page_table = jax.random.permutation(next(ks), n_phys)[:n_used].reshape(DEC_BATCH, n_pages).astype(jnp.int32)
    n_idx = jnp.arange(SSM_STATE, dtype=jnp.float32)
    inp = {}
    inp["x_prompt"] = nrm((BATCH, SEQ, D_MODEL), 1.0)
    inp["x_sample"] = nrm((DEC_BATCH, DEC_SEQ, D_MODEL), 1.0)
    inp["cache_k"] = nrm((L, n_phys, PAGE_SIZE, ATTN_HEADS, HEAD_DIM), 1.0)
    inp["cache_v"] = nrm((L, n_phys, PAGE_SIZE, ATTN_HEADS, HEAD_DIM), 1.0)
    inp["state_ssm_re"] = nrm((L, DEC_BATCH, SSM_GROUPS, SSM_STATE), 1.0)
    inp["state_ssm_im"] = nrm((L, DEC_BATCH, SSM_GROUPS, SSM_STATE), 1.0)
    inp["cache_mem_k"] = nrm((L, DEC_BATCH, MEM_TOKENS, MEM_HEADS, MEM_HEAD_DIM), 1.0)
    inp["cache_mem_v"] = nrm((L, DEC_BATCH, MEM_TOKENS, MEM_HEADS, MEM_HEAD_DIM), 1.0)
    inp["page_table"] = page_table
    inp["mem_prompt"] = nrm((BATCH, MEM_TOKENS, D_MODEL), 1.0)
    inp["g_mix"] = 1.0 + nrm((L, D_MODEL), 0.01)
    inp["w_in"] = nrm((L, D_MODEL, IN_WIDTH), D_MODEL ** -0.5)
    inp["ssm_a_re"] = -0.5 + nrm((L, SSM_GROUPS, SSM_STATE), 0.01)
    inp["ssm_a_im"] = math.pi * n_idx + nrm((L, SSM_GROUPS, SSM_STATE), 0.01)
    inp["ssm_log_dt"] = jax.random.uniform(next(ks), (L, SSM_GROUPS), jnp.float32, math.log(DT_MIN), math.log(DT_MAX))
    inp["ssm_b_re"] = nrm((L, SSM_GROUPS, SSM_STATE, SSM_GROUP), (2 * SSM_GROUP) ** -0.5)
    inp["ssm_b_im"] = nrm((L, SSM_GROUPS, SSM_STATE, SSM_GROUP), (2 * SSM_GROUP) ** -0.5)
    inp["ssm_c_re"] = nrm((L, SSM_GROUPS, SSM_GROUP, SSM_STATE), (2 * SSM_STATE) ** -0.5)
    inp["ssm_c_im"] = nrm((L, SSM_GROUPS, SSM_GROUP, SSM_STATE), (2 * SSM_STATE) ** -0.5)
    inp["ssm_d"] = nrm((L, SSM_WIDTH), 0.5)
    inp["w_glu"] = nrm((L, SSM_WIDTH, SSM_WIDTH), SSM_WIDTH ** -0.5)
    inp["b_glu"] = nrm((L, SSM_WIDTH), 0.01)
    inp["w_out"] = nrm((L, MIX_WIDTH, D_MODEL), MIX_WIDTH ** -0.5)
    inp["g_mem"] = 1.0 + nrm((L, D_MODEL), 0.01)
    inp["w_ck"] = nrm((L, D_MODEL, MEM_WIDTH), D_MODEL ** -0.5)
    inp["w_cv"] = nrm((L, D_MODEL, MEM_WIDTH), D_MODEL ** -0.5)
    inp["g_cross"] = 1.0 + nrm((L, D_MODEL), 0.01)
    inp["w_cq"] = nrm((L, D_MODEL, MEM_WIDTH), D_MODEL ** -0.5)
    inp["w_co"] = nrm((L, MEM_WIDTH, D_MODEL), MEM_WIDTH ** -0.5)
    inp["g_mlp"] = 1.0 + nrm((L, D_MODEL), 0.01)
    inp["w_ff1"] = nrm((L, D_MODEL, D_FF), D_MODEL ** -0.5)
    inp["w_ff2"] = nrm((L, D_FF, D_MODEL), D_FF ** -0.5)
    inp["g_final"] = 1.0 + nrm((D_MODEL,), 0.01)
    return inp


def reference(x_prompt, x_sample, cache_k, cache_v, state_ssm_re, state_ssm_im, cache_mem_k, cache_mem_v,
              page_table, mem_prompt, g_mix, w_in, ssm_a_re, ssm_a_im, ssm_log_dt, ssm_b_re, ssm_b_im,
              ssm_c_re, ssm_c_im, ssm_d, w_glu, b_glu, w_out, g_mem, w_ck, w_cv, g_cross, w_cq, w_co,
              g_mlp, w_ff1, w_ff2, g_final):
    slopes = alibi_slopes()
    hp, hs = x_prompt, x_sample
    kp_l, vp_l, srp_l, sip_l, mkp_l, mvp_l = [], [], [], [], [], []
    ks_l, vs_l, srs_l, sis_l = [], [], [], []
    zeros_state = jnp.zeros((x_prompt.shape[0], SSM_GROUPS, SSM_STATE), jnp.float32)
    for l in range(DEPTH):
        lw = (g_mix[l], w_in[l], ssm_a_re[l], ssm_a_im[l], ssm_log_dt[l], ssm_b_re[l], ssm_b_im[l],
              ssm_c_re[l], ssm_c_im[l], ssm_d[l], w_glu[l], b_glu[l], w_out[l], g_cross[l], w_cq[l],
              w_co[l], g_mlp[l], w_ff1[l], w_ff2[l])
        mk_p, mv_p = memory_kv(mem_prompt, g_mem[l], w_ck[l], w_cv[l])
        hp, k_p, v_p, sr_p, si_p = decoder_layer(
            hp, lambda q, k, v: moba_prompt(q, k, v, slopes), zeros_state, zeros_state, mk_p, mv_p, lw)
        hs, k_s, v_s, sr_s, si_s = decoder_layer(
            hs, lambda q, k, v: moba_paged(q, k, v, cache_k, cache_v, page_table, l, slopes),
            state_ssm_re[l], state_ssm_im[l], cache_mem_k[l], cache_mem_v[l], lw)
        kp_l.append(k_p); vp_l.append(v_p); srp_l.append(sr_p); sip_l.append(si_p)
        mkp_l.append(mk_p); mvp_l.append(mv_p)
        ks_l.append(k_s); vs_l.append(v_s); srs_l.append(sr_s); sis_l.append(si_s)
    y_prompt = rmsnorm(hp, g_final)
    y_sample = rmsnorm(hs, g_final)
    return (y_prompt, y_sample,
            jnp.stack(kp_l), jnp.stack(vp_l), jnp.stack(srp_l), jnp.stack(sip_l),
            jnp.stack(mkp_l), jnp.stack(mvp_l),
            jnp.stack(ks_l), jnp.stack(vs_l), jnp.stack(srs_l), jnp.stack(sis_l))
```

```python
import functools
import math

import jax
import jax.numpy as jnp
from jax import lax
from jax.experimental import pallas as pl
from jax.experimental.pallas import tpu as pltpu

ATTN_HEADS = 8
HEAD_DIM = 64
ATTN_WIDTH = ATTN_HEADS * HEAD_DIM
MOBA_BLOCK = 256
MOBA_TOPK = 3
SSM_GROUP = 16
MEM_HEADS = 4
MEM_HEAD_DIM = 128
RMS_EPS = 1e-6
SEQS_PER_GROUP = 8
LANES = 128
VMEM_LIMIT = 56 * 1024 * 1024

_BF = jnp.bfloat16
_F32 = jnp.float32
_NEG = -1e30
_NT = (((1,), (1,)), ((), ()))


def _slopes():
    return [2.0 ** (-8.0 * (h + 1) / ATTN_HEADS) for h in range(ATTN_HEADS)]


def _rms(x, g):
    return x * lax.rsqrt(jnp.mean(x * x, axis=-1, keepdims=True) + RMS_EPS) * g


def _dot(a, b):
    return jnp.dot(a, b, preferred_element_type=_F32)


def _dot_nt(a, b):
    return lax.dot_general(a, b, _NT, preferred_element_type=_F32)


def _split_bf16(x):
    hi = x.astype(_BF)
    lo = (x - hi.astype(_F32)).astype(_BF)
    return hi, lo


def _dot_nt_3pass(a, b):
    a_hi, a_lo = _split_bf16(a)
    b_hi, b_lo = _split_bf16(b)
    return _dot_nt(a_hi, b_hi) + _dot_nt(a_hi, b_lo) + _dot_nt(a_lo, b_hi)


def _params(sem):
    return pltpu.CompilerParams(dimension_semantics=sem, vmem_limit_bytes=VMEM_LIMIT)


def _in_proj_prompt_kernel(x_ref, g_ref, w_ref, q_ref, k_ref, v_ref, kb_ref, vt_ref, u_ref, km_ref):
    a = ATTN_WIDTH
    hn = _rms(x_ref[...], g_ref[...]).astype(_BF)
    q_ref[...] = _dot(hn, w_ref[:, 0:a])
    k = _dot(hn, w_ref[:, a:2 * a])
    v = _dot(hn, w_ref[:, 2 * a:3 * a])
    k_ref[...] = k
    v_ref[...] = v
    kb_ref[...] = k.astype(_BF)
    u_ref[...] = _dot(hn, w_ref[:, 3 * a:])
    for r in range(k.shape[0] // MOBA_BLOCK):
        rows = slice(r * MOBA_BLOCK, (r + 1) * MOBA_BLOCK)
        vt_ref[r] = v[rows, :].T.astype(_BF)
        km_ref[r] = jnp.mean(k[rows, :], axis=0, keepdims=True)


def _in_proj_prompt(x, g, w, *, tm):
    b, t, d = x.shape
    a = ATTN_WIDTH
    su = w.shape[1] - 3 * a
    nt = t // tm
    nbt = tm // MOBA_BLOCK
    nbg = b // SEQS_PER_GROUP
    row = lambda bi, ti: (bi * nt + ti, 0)
    outs = pl.pallas_call(
        _in_proj_prompt_kernel,
        grid=(b, nt),
        in_specs=[pl.BlockSpec((tm, d), row),
                  pl.BlockSpec((1, d), lambda bi, ti: (0, 0)),
                  pl.BlockSpec(w.shape, lambda bi, ti: (0, 0))],
        out_specs=[pl.BlockSpec((tm, a), row),
                   pl.BlockSpec((tm, a), row),
                   pl.BlockSpec((tm, a), row),
                   pl.BlockSpec((tm, a), row),
                   pl.BlockSpec((None, nbt, a, MOBA_BLOCK), lambda bi, ti: (bi, ti, 0, 0)),
                   pl.BlockSpec((None, tm, su), lambda bi, ti: (bi // SEQS_PER_GROUP, ti, bi % SEQS_PER_GROUP)),
                   pl.BlockSpec((nbt, 1, a), lambda bi, ti: (bi * nt + ti, 0, 0))],
        out_shape=[jax.ShapeDtypeStruct((b * t, a), _F32),
                   jax.ShapeDtypeStruct((b * t, a), _F32),
                   jax.ShapeDtypeStruct((b * t, a), _F32),
                   jax.ShapeDtypeStruct((b * t, a), _BF),
                   jax.ShapeDtypeStruct((b, t // MOBA_BLOCK, a, MOBA_BLOCK), _BF),
                   jax.ShapeDtypeStruct((nbg, t, SEQS_PER_GROUP * su), _F32),
                   jax.ShapeDtypeStruct((b * t // MOBA_BLOCK, 1, a), _F32)],
        compiler_params=_params(("parallel", "parallel")),
        name="in_proj_prompt",
    )(x.reshape(b * t, d), g, w)
    return outs


def _norm_proj_kernel(x_ref, g_ref, w_ref, *o_refs):
    hn = _rms(x_ref[...], g_ref[...]).astype(_BF)
    c0 = 0
    for o_ref in o_refs:
        c1 = c0 + o_ref.shape[1]
        o_ref[...] = _dot(hn, w_ref[:, c0:c1])
        c0 = c1


def _norm_proj(x, g, w, widths, *, tm, name):
    n, d = x.shape
    row = lambda i: (i, 0)
    return pl.pallas_call(
        _norm_proj_kernel,
        grid=(n // tm,),
        in_specs=[pl.BlockSpec((tm, d), row),
                  pl.BlockSpec((1, d), lambda i: (0, 0)),
                  pl.BlockSpec(w.shape, lambda i: (0, 0))],
        out_specs=[pl.BlockSpec((tm, c), row) for c in widths],
        out_shape=[jax.ShapeDtypeStruct((n, c), _F32) for c in widths],
        compiler_params=_params(("parallel",)),
        name=name,
    )(x, g, w)


def _topk_rank_rows(g, rows, n):
    rank = jnp.zeros(g.shape, _F32)
    for m in range(n):
        gm = g[m:m + 1, :]
        beats = (gm > g) | ((gm == g) & (rows > m))
        rank = rank + jnp.where(beats, 1.0, 0.0)
    return rank


def _moba_prompt_kernel(q_ref, k_ref, vt_ref, km_ref, o_ref, sel_ref, acc_ref, *, nb, nbp):
    blk = MOBA_BLOCK
    i = pl.program_id(1)
    q = q_ref[...]
    slopes = _slopes()

    gt = _dot_nt_3pass(km_ref[...], q)
    rows = lax.broadcasted_iota(jnp.int32, (nbp, blk), 0)
    fully_past = rows < i
    for h in range(ATTN_HEADS):
        g = jnp.where(fully_past, gt[h * nbp:(h + 1) * nbp, :], _NEG)
        rank = _topk_rank_rows(g, rows, nb)
        sel = fully_past & (rank < MOBA_TOPK)
        sel_ref[h * nbp:(h + 1) * nbp, :] = jnp.where(sel, 1.0, 0.0)

    qs = (q * (HEAD_DIM ** -0.5)).astype(_BF)
    lane = lax.broadcasted_iota(jnp.int32, (blk, LANES), 1)
    kk = lax.broadcasted_iota(jnp.int32, (blk, blk), 0)
    qq = lax.broadcasted_iota(jnp.int32, (blk, blk), 1)
    d0 = (qq - kk).astype(_F32)
    causal = d0 >= 0.0
    zero = jnp.zeros((), _BF)

    for p in range(ATTN_HEADS // 2):
        cols = slice(p * LANES, (p + 1) * LANES)
        qp = qs[:, cols]
        q_x = (jnp.where(lane < HEAD_DIM, qp, zero), jnp.where(lane >= HEAD_DIM, qp, zero))

        own0 = pl.multiple_of(i * blk, blk)
        kblk = k_ref[pl.ds(own0, blk), cols]
        vtp = vt_ref[i, cols, :]
        carry = []
        pvs = []
        for x in range(2):
            h = 2 * p + x
            s = _dot_nt(kblk, q_x[x]) - slopes[h] * d0
            s = jnp.where(causal, s, _NEG)
            m = jnp.max(s, axis=0, keepdims=True)
            pt = jnp.exp(s - m)
            carry += [m, jnp.sum(pt, axis=0, keepdims=True)]
            pvs.append(_dot(vtp[x * HEAD_DIM:(x + 1) * HEAD_DIM, :], pt.astype(_BF)))
        carry.append(jnp.concatenate(pvs, axis=0))

        def body(j, c):
            start = pl.multiple_of(j * blk, blk)
            kb = k_ref[pl.ds(start, blk), cols]
            vb = vt_ref[j, cols, :]
            off = ((i - j) * blk).astype(_F32)
            new = []
            pv_rows = []
            alpha_rows = []
            for x in range(2):
                h = 2 * p + x
                m_old, l_old = c[2 * x], c[2 * x + 1]
                s = _dot_nt(kb, q_x[x]) - slopes[h] * (d0 + off)
                selrow = sel_ref[pl.ds(h * nbp + j, 1), :]
                s = jnp.where(selrow > 0.5, s, _NEG)
                m_new = jnp.maximum(m_old, jnp.max(s, axis=0, keepdims=True))
                alpha = jnp.exp(m_old - m_new)
                pt = jnp.exp(s - m_new)
                new += [m_new, alpha * l_old + jnp.sum(pt, axis=0, keepdims=True)]
                pv_rows.append(_dot(vb[x * HEAD_DIM:(x + 1) * HEAD_DIM, :], pt.astype(_BF)))
                alpha_rows.append(jnp.broadcast_to(alpha, (HEAD_DIM, blk)))
            acc = c[4] * jnp.concatenate(alpha_rows, axis=0) + jnp.concatenate(pv_rows, axis=0)
            return tuple(new) + (acc,)

        m_e, l_e, m_o, l_o, acc = lax.fori_loop(0, i, body, tuple(carry))
        inv = jnp.concatenate([jnp.broadcast_to(1.0 / l_e, (HEAD_DIM, blk)),
                               jnp.broadcast_to(1.0 / l_o, (HEAD_DIM, blk))], axis=0)
        acc_ref[cols, :] = acc * inv

    o_ref[...] = acc_ref[...].T


def _moba_prompt(q, kb, vt, kmblk):
    b, t, a = kb.shape
    blk = MOBA_BLOCK
    nb = t // blk
    nbp = kmblk.shape[1] // ATTN_HEADS
    return pl.pallas_call(
        functools.partial(_moba_prompt_kernel, nb=nb, nbp=nbp),
        grid=(b, nb),
        in_specs=[pl.BlockSpec((blk, a), lambda bi, i: (bi * nb + i, 0)),
                  pl.BlockSpec((None, t, a), lambda bi, i: (bi, 0, 0)),
                  pl.BlockSpec((None, nb, a, blk), lambda bi, i: (bi, 0, 0, 0)),
                  pl.BlockSpec((None,) + kmblk.shape[1:], lambda bi, i: (bi, 0, 0))],
        out_specs=pl.BlockSpec((blk, a), lambda bi, i: (bi * nb + i, 0)),
        out_shape=jax.ShapeDtypeStruct((b * t, a), _F32),
        scratch_shapes=[pltpu.VMEM((ATTN_HEADS * nbp, blk), _F32),
                        pltpu.VMEM((a, blk), _F32)],
        compiler_params=_params(("parallel", "arbitrary")),
        name="moba_prompt",
    )(q, kb, vt, kmblk)


def _moba_sample_kernel(pt_ref, wq_ref, kn_ref, vn_ref, slope_ref, qpos_ref, *rest,
                        pg, n_pages, page, past, dec):
    del pt_ref
    k_refs = rest[:pg]
    v_refs = rest[pg:2 * pg]
    o_ref, s_ref, km_ref, acc_ref, l_ref = rest[2 * pg:]
    blk = MOBA_BLOCK
    ppb = blk // page
    nb = n_pages // ppb
    n_groups = n_pages // pg
    step = pl.program_id(1)
    wq = wq_ref[...]
    wqb = (wq * (HEAD_DIM ** -0.5)).astype(_BF)

    @pl.when(step < n_groups)
    def _k_phase():
        for c in range(pg // ppb):
            colsum = jnp.zeros((1, wq.shape[1]), _F32)
            for r in range(c * ppb, (c + 1) * ppb):
                kp = k_refs[r][...]
                s_ref[step * pg + r] = _dot_nt(wqb, kp.astype(_BF))
                colsum = colsum + jnp.sum(kp, axis=0, keepdims=True)
            km_ref[pl.ds(step * (pg // ppb) + c, 1), :] = colsum * (1.0 / blk)

    @pl.when(step == n_groups - 1)
    def _softmax():
        nq = wq.shape[0]
        gate = _dot_nt_3pass(wq, km_ref[...])
        cols = lax.broadcasted_iota(jnp.int32, gate.shape, 1)
        rank = jnp.zeros(gate.shape, _F32)
        for m in range(nb):
            gm = gate[:, m:m + 1]
            beats = (gm > gate) | ((gm == gate) & (cols > m))
            rank = rank + jnp.where(beats, 1.0, 0.0)
        sel = jnp.where(rank < MOBA_TOPK, 1.0, 0.0)
        slope = slope_ref[...]
        qpos = qpos_ref[...]
        lane = lax.broadcasted_iota(jnp.int32, (nq, page), 1).astype(_F32)

        mm = jnp.full((nq, page), _NEG, _F32)
        for p in range(n_pages):
            sc = s_ref[p] - slope * (qpos - (p * page + lane))
            sc = jnp.where(sel[:, p // ppb:p // ppb + 1] > 0.5, sc, _NEG)
            s_ref[p] = sc
            mm = jnp.maximum(mm, sc)
        qi = qpos - past
        so = _dot_nt(wqb, kn_ref[...]) - slope * (qi - lane)
        so = jnp.where(lane <= qi, so, _NEG)
        m = jnp.max(jnp.maximum(mm, so), axis=1, keepdims=True)
        ls = jnp.exp(so - m)
        acc_ref[...] = _dot(ls.astype(_BF), vn_ref[...])
        for p in range(n_pages):
            pp = jnp.exp(s_ref[p] - m)
            s_ref[p] = pp
            ls = ls + pp
        l_ref[...] = jnp.sum(ls, axis=1, keepdims=True)

    @pl.when(step >= n_groups)
    def _v_phase():
        acc = acc_ref[...]
        for r in range(pg):
            pp = s_ref[(step - n_groups) * pg + r]
            acc = acc + _dot(pp.astype(_BF), v_refs[r][...].astype(_BF))
        acc_ref[...] = acc

    @pl.when(step == 2 * n_groups - 1)
    def _finish():
        o = acc_ref[...] * (1.0 / l_ref[...])
        lane_a = lax.broadcasted_iota(jnp.int32, (dec, o.shape[1]), 1)
        out = jnp.zeros((dec, o.shape[1]), _F32)
        for h in range(ATTN_HEADS):
            in_head = (lane_a >= h * HEAD_DIM) & (lane_a < (h + 1) * HEAD_DIM)
            out = out + jnp.where(in_head, o[h * dec:(h + 1) * dec, :], 0.0)
        o_ref[...] = out


def _moba_sample(page_table, wq, kn, vn, slope_rows, qpos_rows, cache_k, cache_v, *, pg, past):
    nseq, n_pages = page_table.shape
    _, page, a = cache_k.shape
    nq = wq.shape[1]
    dec = nq // ATTN_HEADS
    n_groups = n_pages // pg
    seq = lambda bi, s, pt: (bi, 0, 0)
    const = lambda bi, s, pt: (0, 0)

    def k_map(r):
        return lambda bi, s, pt: (pt[bi, jnp.minimum(s, n_groups - 1) * pg + r], 0, 0)

    def v_map(r):
        return lambda bi, s, pt: (pt[bi, jnp.maximum(s - n_groups, 0) * pg + r], 0, 0)

    in_specs = [pl.BlockSpec((None, nq, a), seq),
                pl.BlockSpec((None, page, a), seq),
                pl.BlockSpec((None, page, a), seq),
                pl.BlockSpec((nq, 1), const),
                pl.BlockSpec((nq, 1), const)]
    in_specs += [pl.BlockSpec((None, page, a), k_map(r)) for r in range(pg)]
    in_specs += [pl.BlockSpec((None, page, a), v_map(r)) for r in range(pg)]
    return pl.pallas_call(
        functools.partial(_moba_sample_kernel, pg=pg, n_pages=n_pages, page=page, past=past, dec=dec),
        grid_spec=pltpu.PrefetchScalarGridSpec(
            num_scalar_prefetch=1,
            grid=(nseq, 2 * n_groups),
            in_specs=in_specs,
            out_specs=pl.BlockSpec((None, dec, a), seq),
            scratch_shapes=[pltpu.VMEM((n_pages, nq, page), _F32),
                            pltpu.VMEM((n_pages * page // MOBA_BLOCK, a), _F32),
                            pltpu.VMEM((nq, a), _F32),
                            pltpu.VMEM((nq, 1), _F32)]),
        out_shape=jax.ShapeDtypeStruct((nseq, dec, a), _F32),
        compiler_params=_params(("parallel", "arbitrary")),
        name="moba_sample",
    )(page_table, wq, kn, vn, slope_rows, qpos_rows, *([cache_k] * pg), *([cache_v] * pg))


def _s5_disc_kernel(are_ref, aim_ref, ldt_ref, bre_ref, bim_ref, lbr_ref, lbi_ref, bbr_ref, bbi_ref):
    are = are_ref[...]
    aim = aim_ref[...]
    dt = jnp.exp(ldt_ref[...])
    mag = jnp.exp(are * dt)
    ang = aim * dt
    lb_re = mag * jnp.cos(ang)
    lb_im = mag * jnp.sin(ang)
    e_re = lb_re - 1.0
    e_im = lb_im
    den = are * are + aim * aim
    f_re = (e_re * are + e_im * aim) / den
    f_im = (e_im * are - e_re * aim) / den
    br = bre_ref[...]
    bi = bim_ref[...]
    lbr_ref[...] = lb_re
    lbi_ref[...] = lb_im
    bbr_ref[...] = f_re * br - f_im * bi
    bbi_ref[...] = f_re * bi + f_im * br


def _s5_disc(a_re, a_im, log_dt, b_re, b_im):
    g, p, c = b_re.shape
    rep = lambda z: jnp.repeat(z, c, axis=0)
    are, aim = rep(a_re), rep(a_im)
    ldt = rep(log_dt.reshape(g, 1))
    b2 = lambda z: jnp.transpose(z, (0, 2, 1)).reshape(g * c, p)
    shp = jax.ShapeDtypeStruct((g * c, p), _F32)
    lbr, lbi, bbr, bbi = pl.pallas_call(
        _s5_disc_kernel, out_shape=[shp] * 4, name="s5_disc",
    )(are, aim, ldt, b2(b_re), b2(b_im))
    return lbr[::c], lbi[::c], bbr, bbi


def _s5_kernel(u_ref, h0r_ref, h0i_ref, lbr_ref, lbi_ref, bbr_ref, bbi_ref, cr_ref, ci_ref, d_ref,
               wg_ref, bg_ref, y_ref, hr_out, hi_out, hre, him, st_r, st_i, *, tc, cw):
    spg = SEQS_PER_GROUP
    t = pl.program_id(1)
    u = u_ref[...]
    ub = u.astype(_BF)
    hre[...] = _dot(ub, bbr_ref[...])
    him[...] = _dot(ub, bbi_ref[...])

    @pl.when(t == 0)
    def _init():
        st_r[...] = h0r_ref[...]
        st_i[...] = h0i_ref[...]

    ns = hre.shape[1]
    for c in range(ns // cw):
        cs = slice(c * cw, (c + 1) * cw)
        lr = jnp.broadcast_to(lbr_ref[:, cs], (spg, cw))
        li = jnp.broadcast_to(lbi_ref[:, cs], (spg, cw))

        def step(tt, carry):
            sr, si = carry
            r0 = pl.multiple_of(tt * spg, spg)
            nr = lr * sr - li * si + hre[pl.ds(r0, spg), cs]
            ni = lr * si + li * sr + him[pl.ds(r0, spg), cs]
            hre[pl.ds(r0, spg), cs] = nr
            him[pl.ds(r0, spg), cs] = ni
            return nr, ni

        sr, si = lax.fori_loop(0, tc, step, (st_r[:, cs], st_i[:, cs]))
        st_r[:, cs] = sr
        st_i[:, cs] = si

    y = _dot(hre[...].astype(_BF), cr_ref[...]) - _dot(him[...].astype(_BF), ci_ref[...]) + d_ref[...] * u
    z = _dot(y.astype(_BF), wg_ref[...]) + bg_ref[...]
    y_ref[...] = y * (1.0 / (1.0 + jnp.exp(-z)))
    hr_out[...] = st_r[...]
    hi_out[...] = st_i[...]


def _s5(u, h0r, h0i, lbr, lbi, bbr, bbi, cr, ci, d, wg, bg, *, tc):
    nbg, rows, w = u.shape
    spg = SEQS_PER_GROUP
    t = rows // spg
    ns = lbr.shape[1]
    const = lambda gi, ti: (0, 0)
    grp = lambda gi, ti: (gi, 0, 0)
    return pl.pallas_call(
        functools.partial(_s5_kernel, tc=tc, cw=min(ns, 4 * LANES)),
        grid=(nbg, t // tc),
        in_specs=[pl.BlockSpec((None, tc * spg, w), lambda gi, ti: (gi, ti, 0)),
                  pl.BlockSpec((None, spg, ns), grp),
                  pl.BlockSpec((None, spg, ns), grp),
                  pl.BlockSpec((1, ns), const), pl.BlockSpec((1, ns), const),
                  pl.BlockSpec((w, ns), const), pl.BlockSpec((w, ns), const),
                  pl.BlockSpec((ns, w), const), pl.BlockSpec((ns, w), const),
                  pl.BlockSpec((1, w), const),
                  pl.BlockSpec((w, w), const), pl.BlockSpec((1, w), const)],
        out_specs=[pl.BlockSpec((None, tc * spg, w), lambda gi, ti: (gi, ti, 0)),
                   pl.BlockSpec((None, spg, ns), grp),
                   pl.BlockSpec((None, spg, ns), grp)],
        out_shape=[jax.ShapeDtypeStruct((nbg, rows, w), _F32),
                   jax.ShapeDtypeStruct((nbg, spg, ns), _F32),
                   jax.ShapeDtypeStruct((nbg, spg, ns), _F32)],
        scratch_shapes=[pltpu.VMEM((tc * spg, ns), _F32), pltpu.VMEM((tc * spg, ns), _F32),
                        pltpu.VMEM((spg, ns), _F32), pltpu.VMEM((spg, ns), _F32)],
        compiler_params=_params(("parallel", "arbitrary")),
        name="s5",
    )(u, h0r, h0i, lbr, lbi, bbr, bbi, cr, ci, d, wg, bg)


def _block_diag(blocks):
    g, r, c = blocks.shape
    eye = jnp.eye(g, dtype=blocks.dtype)
    return (blocks[:, :, None, :] * eye[:, None, :, None]).reshape(g * r, g * c)


def _mix_kernel(x_ref, a_ref, s_ref, woa_ref, wos_ref, gc_ref, wcq_ref, x1_ref, cq_ref):
    x1 = (x_ref[...] + _dot(a_ref[...].astype(_BF), woa_ref[...])
          + _dot(s_ref[...].astype(_BF), wos_ref[...]))
    x1_ref[...] = x1
    cq_ref[...] = _dot(_rms(x1, gc_ref[...]).astype(_BF), wcq_ref[...])


def _mix(x, attn, ssm, ssm_map, woa, wos, gc, wcq, *, nb, nt, tm):
    n, d = x.shape
    row = lambda bi, ti: (bi * nt + ti, 0)
    const = lambda bi, ti: (0, 0)
    mw = wcq.shape[1]
    sw = wos.shape[0]
    return pl.pallas_call(
        _mix_kernel,
        grid=(nb, nt),
        in_specs=[pl.BlockSpec((tm, d), row),
                  pl.BlockSpec((tm, attn.shape[1]), row),
                  pl.BlockSpec((tm, sw), ssm_map),
                  pl.BlockSpec(woa.shape, const), pl.BlockSpec(wos.shape, const),
                  pl.BlockSpec((1, d), const), pl.BlockSpec(wcq.shape, const)],
        out_specs=[pl.BlockSpec((tm, d), row), pl.BlockSpec((tm, mw), row)],
        out_shape=[jax.ShapeDtypeStruct((n, d), _F32), jax.ShapeDtypeStruct((n, mw), _F32)],
        compiler_params=_params(("parallel", "parallel")),
        name="mix",
    )(x, attn, ssm, woa, wos, gc, wcq)


def _cross_attn_kernel(q_ref, mk_ref, mv_ref, o_ref):
    q = q_ref[...]
    for h in range(MEM_HEADS):
        cols = slice(h * MEM_HEAD_DIM, (h + 1) * MEM_HEAD_DIM)
        s = _dot_nt(q[:, cols].astype(_BF), mk_ref[:, cols].astype(_BF)) * (MEM_HEAD_DIM ** -0.5)
        m = jnp.max(s, axis=-1, keepdims=True)
        e = jnp.exp(s - m)
        p = e * (1.0 / jnp.sum(e, axis=-1, keepdims=True))
        o_ref[:, cols] = _dot(p.astype(_BF), mv_ref[:, cols].astype(_BF))


def _cross_attn(cq, mk, mv, *, nb, nt, tq):
    n, mw = cq.shape
    m = mk.shape[1]
    row = lambda bi, ti: (bi * nt + ti, 0)
    mem = lambda bi, ti: (bi, 0, 0)
    return pl.pallas_call(
        _cross_attn_kernel,
        grid=(nb, nt),
        in_specs=[pl.BlockSpec((tq, mw), row),
                  pl.BlockSpec((None, m, mw), mem), pl.BlockSpec((None, m, mw), mem)],
        out_specs=pl.BlockSpec((tq, mw), row),
        out_shape=jax.ShapeDtypeStruct((n, mw), _F32),
        compiler_params=_params(("parallel", "parallel")),
        name="cross_attn",
    )(cq, mk, mv)


def _mlp_kernel(x1_ref, o_ref, wco_ref, gm_ref, w1_ref, w2_ref, gf_ref, y_ref, *, fc):
    x2 = x1_ref[...] + _dot(o_ref[...].astype(_BF), wco_ref[...])
    hn = _rms(x2, gm_ref[...]).astype(_BF)
    acc = x2
    for c in range(w1_ref.shape[1] // fc):
        cols = slice(c * fc, (c + 1) * fc)
        hid = jnp.square(jnp.maximum(_dot(hn, w1_ref[:, cols]), 0.0)).astype(_BF)
        acc = acc + _dot(hid, w2_ref[cols, :])
    y_ref[...] = _rms(acc, gf_ref[...])


def _mlp(x1, o, wco, gm, w1, w2, gf, *, tm):
    n, d = x1.shape
    row = lambda i: (i, 0)
    const = lambda i: (0, 0)
    return pl.pallas_call(
        functools.partial(_mlp_kernel, fc=min(w1.shape[1], 1024)),
        grid=(n // tm,),
        in_specs=[pl.BlockSpec((tm, d), row), pl.BlockSpec((tm, o.shape[1]), row),
                  pl.BlockSpec(wco.shape, const), pl.BlockSpec((1, d), const),
                  pl.BlockSpec(w1.shape, const), pl.BlockSpec(w2.shape, const),
                  pl.BlockSpec((1, d), const)],
        out_specs=pl.BlockSpec((tm, d), row),
        out_shape=jax.ShapeDtypeStruct((n, d), _F32),
        compiler_params=_params(("parallel",)),
        name="mlp",
    )(x1, o, wco, gm, w1, w2, gf)


def kernel(x_prompt, x_sample, cache_k, cache_v, state_ssm_re, state_ssm_im, cache_mem_k, cache_mem_v,
           page_table, mem_prompt, g_mix, w_in, ssm_a_re, ssm_a_im, ssm_log_dt, ssm_b_re, ssm_b_im,
           ssm_c_re, ssm_c_im, ssm_d, w_glu, b_glu, w_out, g_mem, w_ck, w_cv, g_cross, w_cq, w_co,
           g_mlp, w_ff1, w_ff2, g_final):
    depth = w_in.shape[0]
    assert depth == 1, "single-layer trunk"
    b, t, d = x_prompt.shape
    nseq, dec, _ = x_sample.shape
    a = ATTN_WIDTH
    spg = SEQS_PER_GROUP
    blk = MOBA_BLOCK
    assert t % blk == 0 and b % spg == 0 and nseq % spg == 0 and dec == spg
    n_phys, page = cache_k.shape[1], cache_k.shape[2]
    n_pages = page_table.shape[1]
    past = n_pages * page
    assert past % blk == 0 and blk % page == 0 and dec <= page
    nb = t // blk
    nbp = -(-nb // 8) * 8
    groups, ns_p = ssm_a_re.shape[1], ssm_a_re.shape[2]
    ns = groups * ns_p
    sw = groups * SSM_GROUP
    mtok = mem_prompt.shape[1]
    mw = MEM_HEADS * MEM_HEAD_DIM
    row2 = lambda z: z.reshape(1, -1)

    w_in_b = w_in[0].astype(_BF)
    wo_a = w_out[0, :a].astype(_BF)
    wo_s = w_out[0, a:].astype(_BF)
    w_cq_b, w_co_b = w_cq[0].astype(_BF), w_co[0].astype(_BF)
    w1_b, w2_b = w_ff1[0].astype(_BF), w_ff2[0].astype(_BF)
    w_glu_b = w_glu[0].astype(_BF)
    w_ckv_b = jnp.concatenate([w_ck[0], w_cv[0]], axis=1).astype(_BF)
    lbr, lbi, bbr, bbi = _s5_disc(ssm_a_re[0], ssm_a_im[0], ssm_log_dt[0], ssm_b_re[0], ssm_b_im[0])
    lbr, lbi = row2(lbr), row2(lbi)
    bb_r = _block_diag(bbr.reshape(groups, SSM_GROUP, ns_p)).astype(_BF)
    bb_i = _block_diag(bbi.reshape(groups, SSM_GROUP, ns_p)).astype(_BF)
    c_r = _block_diag(jnp.transpose(ssm_c_re[0], (0, 2, 1))).astype(_BF)
    c_i = _block_diag(jnp.transpose(ssm_c_im[0], (0, 2, 1))).astype(_BF)
    s5_w = (lbr, lbi, bb_r, bb_i, c_r, c_i, row2(ssm_d[0]), w_glu_b, row2(b_glu[0]))

    tm = 512 if t % 512 == 0 else blk
    q_p, k_p, v_p, kb_p, vt_p, u_p, km_p = _in_proj_prompt(x_prompt, row2(g_mix[0]), w_in_b, tm=tm)
    km = km_p.reshape(b, nb, ATTN_HEADS, HEAD_DIM)
    km = jnp.pad(jnp.transpose(km, (0, 2, 1, 3)), ((0, 0), (0, 0), (0, nbp - nb), (0, 0)))
    kmblk = (km[:, :, :, None, :] * jnp.eye(ATTN_HEADS, dtype=_F32)[None, :, None, :, None]
             ).reshape(b, ATTN_HEADS * nbp, a)
    attn_p = _moba_prompt(q_p, kb_p.reshape(b, t, a), vt_p, kmblk)

    zeros_state = jnp.zeros((b // spg, spg, ns), _F32)
    tc = 64 if t % 64 == 0 else t
    ssm_p, hr_p, hi_p = _s5(u_p.reshape(b // spg, t * spg, sw), zeros_state, zeros_state, *s5_w, tc=tc)

    nt = t // tm
    ssm_map_p = lambda bi, ti: (bi // spg * nt + ti, bi % spg)
    x1_p, cq_p = _mix(x_prompt.reshape(b * t, d), attn_p, ssm_p.reshape(b // spg * t, spg * sw), ssm_map_p,
                      wo_a, wo_s, row2(g_cross[0]), w_cq_b, nb=b, nt=nt, tm=tm)
    mk_p, mv_p = _norm_proj(mem_prompt.reshape(b * mtok, d), row2(g_mem[0]), w_ckv_b, (mw, mw),
                            tm=mtok, name="mem_kv")
    o_p = _cross_attn(cq_p, mk_p.reshape(b, mtok, mw), mv_p.reshape(b, mtok, mw), nb=b, nt=nt, tq=tm)
    y_p = _mlp(x1_p, o_p, w_co_b, row2(g_mlp[0]), w1_b, w2_b, row2(g_final), tm=tm)

    n_s = nseq * dec
    tms = 256 if n_s % 256 == 0 else n_s
    q_s, k_s, v_s, u_s = _norm_proj(x_sample.reshape(n_s, d), row2(g_mix[0]), w_in_b, (a, a, a, sw),
                                    tm=tms, name="in_proj_sample")
    nq = ATTN_HEADS * dec
    head_of_row = jnp.arange(nq) // dec
    head_of_col = jnp.arange(a) // HEAD_DIM
    head_mask = (head_of_row[:, None] == head_of_col[None, :]).astype(_F32)
    wq = jnp.tile(q_s.reshape(nseq, dec, a), (1, ATTN_HEADS, 1)) * head_mask[None]
    pad_new = lambda z: jnp.pad(z.reshape(nseq, dec, a), ((0, 0), (0, page - dec), (0, 0))).astype(_BF)
    slope_rows = jnp.asarray(_slopes(), _F32)[head_of_row].reshape(nq, 1)
    qpos_rows = (past + jnp.arange(nq) % dec).astype(_F32).reshape(nq, 1)
    pg = 8 if n_pages % 8 == 0 else blk // page
    attn_s = _moba_sample(page_table, wq, pad_new(k_s), pad_new(v_s), slope_rows, qpos_rows,
                          cache_k[0].reshape(n_phys, page, a), cache_v[0].reshape(n_phys, page, a),
                          pg=pg, past=past)

    to_groups = lambda z: jnp.transpose(z.reshape(nseq // spg, spg, dec, -1), (0, 2, 1, 3))
    u_g = to_groups(u_s).reshape(nseq // spg, dec * spg, sw)
    h0r = state_ssm_re[0].reshape(nseq // spg, spg, ns)
    h0i = state_ssm_im[0].reshape(nseq // spg, spg, ns)
    ssm_g, hr_s, hi_s = _s5(u_g, h0r, h0i, *s5_w, tc=dec)
    ssm_s = jnp.transpose(ssm_g.reshape(nseq // spg, dec, spg, sw), (0, 2, 1, 3)).reshape(n_s, sw)

    nts = n_s // tms
    x1_s, cq_s = _mix(x_sample.reshape(n_s, d), attn_s.reshape(n_s, a), ssm_s, lambda bi, ti: (ti, 0),
                      wo_a, wo_s, row2(g_cross[0]), w_cq_b, nb=1, nt=nts, tm=tms)
    o_s = _cross_attn(cq_s, cache_mem_k[0].reshape(nseq, mtok, mw), cache_mem_v[0].reshape(nseq, mtok, mw),
                      nb=nseq, nt=1, tq=dec)
    y_s = _mlp(x1_s, o_s, w_co_b, row2(g_mlp[0]), w1_b, w2_b, row2(g_final), tm=tms)

    hd = (ATTN_HEADS, HEAD_DIM)
    return (y_p.reshape(b, t, d), y_s.reshape(nseq, dec, d),
            k_p.reshape((1, b, t) + hd), v_p.reshape((1, b, t) + hd),
            hr_p.reshape(1, b, groups, ns_p), hi_p.reshape(1, b, groups, ns_p),
            mk_p.reshape(1, b, mtok, MEM_HEADS, MEM_HEAD_DIM), mv_p.reshape(1, b, mtok, MEM_HEADS, MEM_HEAD_DIM),
            k_s.reshape((1, nseq, dec) + hd), v_s.reshape((1, nseq, dec) + hd),
            hr_s.reshape(1, nseq, groups, ns_p), hi_s.reshape(1, nseq, groups, ns_p))
```

```python
import functools

import jax
import jax.numpy as jnp
from jax import lax
from jax.experimental import pallas as pl
from jax.experimental.pallas import tpu as pltpu

ATTN_HEADS = 8
HEAD_DIM = 64
ATTN_WIDTH = ATTN_HEADS * HEAD_DIM
MOBA_BLOCK = 256
MOBA_TOPK = 3
SSM_GROUP = 16
MEM_HEADS = 4
MEM_HEAD_DIM = 128
RMS_EPS = 1e-6
SEQS_PER_GROUP = 8
LANES = 128
MXU_N = 256
VMEM_LIMIT = 56 * 1024 * 1024

_BF = jnp.bfloat16
_F32 = jnp.float32
_NEG = -1e30
_BIG = 1e30
_LOG2E = 1.4426950408889634
_NT = (((1,), (1,)), ((), ()))


def _slopes():
    return [2.0 ** (-8.0 * (h + 1) / ATTN_HEADS) for h in range(ATTN_HEADS)]


def _rms(x, g):
    return x * lax.rsqrt(jnp.mean(x * x, axis=-1, keepdims=True) + RMS_EPS) * g


def _dot(a, b):
    return jnp.dot(a, b, preferred_element_type=_F32)


def _dot_nt(a, b):
    return lax.dot_general(a, b, _NT, preferred_element_type=_F32)


def _split_bf16(x):
    hi = x.astype(_BF)
    lo = (x - hi.astype(_F32)).astype(_BF)
    return hi, lo


def _dot_nt_3pass(a, b):
    a_hi, a_lo = _split_bf16(a)
    b_hi, b_lo = _split_bf16(b)
    return _dot_nt(a_hi, b_hi) + _dot_nt(a_hi, b_lo) + _dot_nt(a_lo, b_hi)


def _rank_update(rank, gm, g, tie):
    return rank + jnp.where(gm > g, 1.0, jnp.where(gm == g, tie, 0.0))


def _params(sem):
    return pltpu.CompilerParams(dimension_semantics=sem, vmem_limit_bytes=VMEM_LIMIT)


def _in_proj_prompt_kernel(x_ref, g_ref, w_ref, q_ref, kt_ref, vt_ref, kb_ref, vtb_ref, u_ref, km_ref):
    a = ATTN_WIDTH
    hn = _rms(x_ref[...], g_ref[...]).astype(_BF)
    q_ref[...] = _dot(hn, w_ref[:, 0:a])
    k = _dot(hn, w_ref[:, a:2 * a])
    v = _dot(hn, w_ref[:, 2 * a:3 * a])
    kt_ref[...] = k.T
    vt = v.T
    vt_ref[...] = vt
    kb_ref[...] = k.astype(_BF)
    u_ref[...] = _dot(hn, w_ref[:, 3 * a:])
    for r in range(k.shape[0] // MOBA_BLOCK):
        rows = slice(r * MOBA_BLOCK, (r + 1) * MOBA_BLOCK)
        vtb_ref[r] = vt[:, rows].astype(_BF)
        km_ref[r] = jnp.mean(k[rows, :], axis=0, keepdims=True)


def _in_proj_prompt(x, g, w, *, tm):
    b, t, d = x.shape
    a = ATTN_WIDTH
    su = w.shape[1] - 3 * a
    nt = t // tm
    nbt = tm // MOBA_BLOCK
    nbg = b // SEQS_PER_GROUP
    row = lambda bi, ti: (bi * nt + ti, 0)
    tok_minor = lambda bi, ti: (bi, 0, ti)
    return pl.pallas_call(
        _in_proj_prompt_kernel,
        grid=(b, nt),
        in_specs=[pl.BlockSpec((tm, d), row),
                  pl.BlockSpec((1, d), lambda bi, ti: (0, 0)),
                  pl.BlockSpec(w.shape, lambda bi, ti: (0, 0))],
        out_specs=[pl.BlockSpec((tm, a), row),
                   pl.BlockSpec((None, a, tm), tok_minor),
                   pl.BlockSpec((None, a, tm), tok_minor),
                   pl.BlockSpec((tm, a), row),
                   pl.BlockSpec((None, nbt, a, MOBA_BLOCK), lambda bi, ti: (bi, ti, 0, 0)),
                   pl.BlockSpec((None, tm, su), lambda bi, ti: (bi // SEQS_PER_GROUP, ti, bi % SEQS_PER_GROUP)),
                   pl.BlockSpec((nbt, 1, a), lambda bi, ti: (bi * nt + ti, 0, 0))],
        out_shape=[jax.ShapeDtypeStruct((b * t, a), _F32),
                   jax.ShapeDtypeStruct((b, a, t), _F32),
                   jax.ShapeDtypeStruct((b, a, t), _F32),
                   jax.ShapeDtypeStruct((b * t, a), _BF),
                   jax.ShapeDtypeStruct((b, t // MOBA_BLOCK, a, MOBA_BLOCK), _BF),
                   jax.ShapeDtypeStruct((nbg, t, SEQS_PER_GROUP * su), _F32),
                   jax.ShapeDtypeStruct((b * t // MOBA_BLOCK, 1, a), _F32)],
        compiler_params=_params(("parallel", "parallel")),
        name="in_proj_prompt",
    )(x.reshape(b * t, d), g, w)


def _norm_proj_kernel(x_ref, g_ref, w_ref, *o_refs):
    hn = _rms(x_ref[...], g_ref[...]).astype(_BF)
    c0 = 0
    for o_ref in o_refs:
        c1 = c0 + o_ref.shape[1]
        o_ref[...] = _dot(hn, w_ref[:, c0:c1])
        c0 = c1


def _norm_proj(x, g, w, widths, *, tm, name):
    n, d = x.shape
    row = lambda i: (i, 0)
    return pl.pallas_call(
        _norm_proj_kernel,
        grid=(n // tm,),
        in_specs=[pl.BlockSpec((tm, d), row),
                  pl.BlockSpec((1, d), lambda i: (0, 0)),
                  pl.BlockSpec(w.shape, lambda i: (0, 0))],
        out_specs=[pl.BlockSpec((tm, c), row) for c in widths],
        out_shape=[jax.ShapeDtypeStruct((n, c), _F32) for c in widths],
        compiler_params=_params(("parallel",)),
        name=name,
    )(x, g, w)


def _moba_prompt_kernel(q_ref, k_ref, vt_ref, km_ref, o_ref,
                        sel_ref, bias_ref, qx_ref, s_ref, m_ref, l_ref, acc_ref, *, nb, nbp):
    blk = MOBA_BLOCK
    nh = ATTN_HEADS
    i = pl.program_id(1)
    q = q_ref[...]
    slopes2 = [s * _LOG2E for s in _slopes()]

    gt = _dot_nt_3pass(km_ref[...], q)
    rows = lax.broadcasted_iota(jnp.int32, (nbp, blk), 0)
    fully_past = rows < i
    g = [jnp.where(fully_past, gt[h * nbp:(h + 1) * nbp, :], _NEG) for h in range(nh)]
    rank = [jnp.zeros((nbp, blk), _F32) for _ in range(nh)]
    for m in range(nb):
        tie = jnp.where(rows > m, 1.0, 0.0)
        for h in range(nh):
            rank[h] = _rank_update(rank[h], g[h][m:m + 1, :], g[h], tie)
    for h in range(nh):
        sel_ref[h * nbp:(h + 1) * nbp, :] = jnp.where(fully_past & (rank[h] < MOBA_TOPK), 1.0, 0.0)

    qs = (q * (HEAD_DIM ** -0.5 * _LOG2E)).astype(_BF)
    lane = lax.broadcasted_iota(jnp.int32, (blk, LANES), 1)
    zero = jnp.zeros((), _BF)
    for p in range(nh // 2):
        qp = qs[:, p * LANES:(p + 1) * LANES]
        qx_ref[2 * p] = jnp.where(lane < HEAD_DIM, qp, zero)
        qx_ref[2 * p + 1] = jnp.where(lane >= HEAD_DIM, qp, zero)

    kk = lax.broadcasted_iota(jnp.int32, (blk, blk), 0)
    qq = lax.broadcasted_iota(jnp.int32, (blk, blk), 1)
    dkq = (kk - qq).astype(_F32)
    for h in range(nh):
        bias_ref[h] = slopes2[h] * dkq
    causal_neg = jnp.where(dkq <= 0.0, 0.0, _NEG)

    def process(j, own):
        start = pl.multiple_of(j * blk, blk)
        coff = None if own else ((i - j) * blk).astype(_F32)
        for p in range(nh // 2):
            cols = slice(p * LANES, (p + 1) * LANES)
            kb = k_ref[pl.ds(start, blk), cols]
            for x in range(2):
                h = 2 * p + x
                s_ref[h] = _dot_nt(kb, qx_ref[h])
        for p in range(nh // 2):
            cols = slice(p * LANES, (p + 1) * LANES)
            vb = vt_ref[j, cols, :]
            for x in range(2):
                h = 2 * p + x
                arow = slice(h * HEAD_DIM, (h + 1) * HEAD_DIM)
                s = s_ref[h] + bias_ref[h]
                if own:
                    s = s + causal_neg
                    m_new = jnp.max(s, axis=0, keepdims=True)
                    t = m_new
                else:
                    c = slopes2[h] * coff
                    picked = sel_ref[pl.ds(h * nbp + j, 1), :] > 0.5
                    m_old = m_ref[h]
                    m_new = jnp.where(picked, jnp.maximum(m_old, jnp.max(s, axis=0, keepdims=True) - c), m_old)
                    t = jnp.where(picked, m_new + c, _BIG)
                    alpha = jnp.exp2(m_old - m_new)
                pt = jnp.exp2(s - t)
                psum = jnp.sum(pt, axis=0, keepdims=True)
                pv = _dot(vb[x * HEAD_DIM:(x + 1) * HEAD_DIM, :], pt.astype(_BF))
                if own:
                    l_ref[h] = psum
                    acc_ref[arow, :] = pv
                else:
                    l_ref[h] = alpha * l_ref[h] + psum
                    acc_ref[arow, :] = acc_ref[arow, :] * alpha + pv
                m_ref[h] = m_new

    process(i, True)

    def body(j, carry):
        process(j, False)
        return carry

    lax.fori_loop(0, i, body, 0)

    for h in range(nh):
        arow = slice(h * HEAD_DIM, (h + 1) * HEAD_DIM)
        acc_ref[arow, :] = acc_ref[arow, :] * (1.0 / l_ref[h])
    o_ref[...] = acc_ref[...].T


def _moba_prompt(q, kb, vt, kmblk):
    b, t, a = kb.shape
    blk = MOBA_BLOCK
    nb = t // blk
    nbp = kmblk.shape[1] // ATTN_HEADS
    return pl.pallas_call(
        functools.partial(_moba_prompt_kernel, nb=nb, nbp=nbp),
        grid=(b, nb),
        in_specs=[pl.BlockSpec((blk, a), lambda bi, i: (bi * nb + i, 0)),
                  pl.BlockSpec((None, t, a), lambda bi, i: (bi, 0, 0)),
                  pl.BlockSpec((None, nb, a, blk), lambda bi, i: (bi, 0, 0, 0)),
                  pl.BlockSpec((None,) + kmblk.shape[1:], lambda bi, i: (bi, 0, 0))],
        out_specs=pl.BlockSpec((blk, a), lambda bi, i: (bi * nb + i, 0)),
        out_shape=jax.ShapeDtypeStruct((b * t, a), _F32),
        scratch_shapes=[pltpu.VMEM((ATTN_HEADS * nbp, blk), _F32),
                        pltpu.VMEM((ATTN_HEADS, blk, blk), _F32),
                        pltpu.VMEM((ATTN_HEADS, blk, LANES), _BF),
                        pltpu.VMEM((ATTN_HEADS, blk, blk), _F32),
                        pltpu.VMEM((ATTN_HEADS, 1, blk), _F32),
                        pltpu.VMEM((ATTN_HEADS, 1, blk), _F32),
                        pltpu.VMEM((a, blk), _F32)],
        compiler_params=_params(("parallel", "arbitrary")),
        name="moba_prompt",
    )(q, kb, vt, kmblk)


def _moba_sample_kernel(pt_ref, wq_ref, kn_ref, vn_ref, slope_ref, qpos_ref, *rest,
                        pg, n_pages, page, past, dec):
    del pt_ref
    k_refs = rest[:pg]
    v_refs = rest[pg:2 * pg]
    o_ref, s_ref, gate_ref, acc_ref, l_ref = rest[2 * pg:]
    blk = MOBA_BLOCK
    ppb = blk // page
    nb = n_pages // ppb
    n_groups = n_pages // pg
    step = pl.program_id(1)
    wq = wq_ref[...]
    nq = wq.shape[0]
    wqb = (wq * (HEAD_DIM ** -0.5)).astype(_BF)
    blk_lane = lax.broadcasted_iota(jnp.int32, (nq, LANES), 1)

    @pl.when(step == 0)
    def _init():
        gate_ref[...] = jnp.zeros(gate_ref.shape, _F32)

    @pl.when(step < n_groups)
    def _k_phase():
        for c in range(pg // ppb):
            gs = jnp.zeros((nq, page), _F32)
            for r in range(c * ppb, (c + 1) * ppb):
                st = _dot(wqb, k_refs[r][...].astype(_BF))
                s_ref[step * pg + r] = st
                gs = gs + st
            n = step * (pg // ppb) + c
            gsum = jnp.sum(gs, axis=1, keepdims=True)
            gate_ref[...] = jnp.where(blk_lane == n, gsum, gate_ref[...])

    @pl.when(step == n_groups - 1)
    def _softmax():
        gate = gate_ref[...]
        rank = jnp.zeros(gate.shape, _F32)
        for m in range(nb):
            tie = jnp.where(blk_lane > m, 1.0, 0.0)
            rank = _rank_update(rank, gate[:, m:m + 1], gate, tie)
        sel = jnp.where(rank < MOBA_TOPK, 1.0, 0.0)
        slope = slope_ref[...]
        qpos = qpos_ref[...]
        lane = lax.broadcasted_iota(jnp.int32, (nq, page), 1).astype(_F32)

        mm = jnp.full((nq, page), _NEG, _F32)
        for p in range(n_pages):
            sc = s_ref[p] - slope * (qpos - (p * page + lane))
            sc = jnp.where(sel[:, p // ppb:p // ppb + 1] > 0.5, sc, _NEG)
            s_ref[p] = sc
            mm = jnp.maximum(mm, sc)
        qi = qpos - past
        so = _dot_nt(wqb, kn_ref[...]) - slope * (qi - lane)
        so = jnp.where(lane <= qi, so, _NEG)
        m = jnp.max(jnp.maximum(mm, so), axis=1, keepdims=True)
        ls = jnp.exp(so - m)
        acc_ref[...] = _dot(ls.astype(_BF), vn_ref[...])
        for p in range(n_pages):
            pp = jnp.exp(s_ref[p] - m)
            s_ref[p] = pp
            ls = ls + pp
        l_ref[...] = jnp.sum(ls, axis=1, keepdims=True)

    @pl.when(step >= n_groups)
    def _v_phase():
        acc = acc_ref[...]
        for r in range(pg):
            pp = s_ref[(step - n_groups) * pg + r]
            acc = acc + _dot_nt(pp.astype(_BF), v_refs[r][...].astype(_BF))
        acc_ref[...] = acc

    @pl.when(step == 2 * n_groups - 1)
    def _finish():
        o = acc_ref[...] * (1.0 / l_ref[...])
        lane_a = lax.broadcasted_iota(jnp.int32, (dec, o.shape[1]), 1)
        out = jnp.zeros((dec, o.shape[1]), _F32)
        for h in range(ATTN_HEADS):
            in_head = (lane_a >= h * HEAD_DIM) & (lane_a < (h + 1) * HEAD_DIM)
            out = out + jnp.where(in_head, o[h * dec:(h + 1) * dec, :], 0.0)
        o_ref[...] = out


def _moba_sample(page_table, wq, kn, vn, slope_rows, qpos_rows, cache_kt, cache_vt, *, pg, past):
    nseq, n_pages = page_table.shape
    _, a, page = cache_kt.shape
    nq = wq.shape[1]
    dec = nq // ATTN_HEADS
    n_groups = n_pages // pg
    assert n_pages * page // MOBA_BLOCK <= LANES
    seq = lambda bi, s, pt: (bi, 0, 0)
    const = lambda bi, s, pt: (0, 0)

    def k_map(r):
        return lambda bi, s, pt: (pt[bi, jnp.minimum(s, n_groups - 1) * pg + r], 0, 0)

    def v_map(r):
        return lambda bi, s, pt: (pt[bi, jnp.maximum(s - n_groups, 0) * pg + r], 0, 0)

    in_specs = [pl.BlockSpec((None, nq, a), seq),
                pl.BlockSpec((None, page, a), seq),
                pl.BlockSpec((None, page, a), seq),
                pl.BlockSpec((nq, 1), const),
                pl.BlockSpec((nq, 1), const)]
    in_specs += [pl.BlockSpec((None, a, page), k_map(r)) for r in range(pg)]
    in_specs += [pl.BlockSpec((None, a, page), v_map(r)) for r in range(pg)]
    return pl.pallas_call(
        functools.partial(_moba_sample_kernel, pg=pg, n_pages=n_pages, page=page, past=past, dec=dec),
        grid_spec=pltpu.PrefetchScalarGridSpec(
            num_scalar_prefetch=1,
            grid=(nseq, 2 * n_groups),
            in_specs=in_specs,
            out_specs=pl.BlockSpec((None, dec, a), seq),
            scratch_shapes=[pltpu.VMEM((n_pages, nq, page), _F32),
                            pltpu.VMEM((nq, LANES), _F32),
                            pltpu.VMEM((nq, a), _F32),
                            pltpu.VMEM((nq, 1), _F32)]),
        out_shape=jax.ShapeDtypeStruct((nseq, dec, a), _F32),
        compiler_params=_params(("parallel", "arbitrary")),
        name="moba_sample",
    )(page_table, wq, kn, vn, slope_rows, qpos_rows, *([cache_kt] * pg), *([cache_vt] * pg))


def _s5_disc_kernel(are_ref, aim_ref, ldt_ref, bre_ref, bim_ref, lbr_ref, lbi_ref, bbr_ref, bbi_ref):
    are = are_ref[...]
    aim = aim_ref[...]
    dt = jnp.exp(ldt_ref[...])
    mag = jnp.exp(are * dt)
    ang = aim * dt
    lb_re = mag * jnp.cos(ang)
    lb_im = mag * jnp.sin(ang)
    e_re = lb_re - 1.0
    e_im = lb_im
    den = are * are + aim * aim
    f_re = (e_re * are + e_im * aim) / den
    f_im = (e_im * are - e_re * aim) / den
    br = bre_ref[...]
    bi = bim_ref[...]
    lbr_ref[...] = lb_re
    lbi_ref[...] = lb_im
    bbr_ref[...] = f_re * br - f_im * bi
    bbi_ref[...] = f_re * bi + f_im * br


def _s5_disc(a_re, a_im, log_dt, b_re, b_im):
    g, p, c = b_re.shape
    rep = lambda z: jnp.repeat(z, c, axis=0)
    are, aim = rep(a_re), rep(a_im)
    ldt = rep(log_dt.reshape(g, 1))
    b2 = lambda z: jnp.transpose(z, (0, 2, 1)).reshape(g * c, p)
    shp = jax.ShapeDtypeStruct((g * c, p), _F32)
    lbr, lbi, bbr, bbi = pl.pallas_call(
        _s5_disc_kernel, out_shape=[shp] * 4, name="s5_disc",
    )(are, aim, ldt, b2(b_re), b2(b_im))
    return lbr[::c], lbi[::c], bbr, bbi


def _s5_kernel(u_ref, h0r_ref, h0i_ref, lbr_ref, lbi_ref, bbr_ref, bbi_ref, cr_ref, ci_ref, d_ref,
               wg_ref, bg_ref, y_ref, hr_out, hi_out, hre, him, st_r, st_i, *, tc, cw):
    spg = SEQS_PER_GROUP
    t = pl.program_id(1)
    u = u_ref[...]
    ub = u.astype(_BF)
    w = u.shape[1]
    ns = hre.shape[1]
    ratio = ns // w
    for n in range(ns // MXU_N):
        out = slice(n * MXU_N, (n + 1) * MXU_N)
        lo = (n * MXU_N // ratio) // LANES * LANES
        src = slice(lo, lo + LANES)
        hre[:, out] = _dot(ub[:, src], bbr_ref[src, out])
        him[:, out] = _dot(ub[:, src], bbi_ref[src, out])

    @pl.when(t == 0)
    def _init():
        st_r[...] = h0r_ref[...]
        st_i[...] = h0i_ref[...]

    for c in range(ns // cw):
        cs = slice(c * cw, (c + 1) * cw)
        lr = jnp.broadcast_to(lbr_ref[:, cs], (spg, cw))
        li = jnp.broadcast_to(lbi_ref[:, cs], (spg, cw))

        def step(tt, carry):
            sr, si = carry
            r0 = pl.multiple_of(tt * spg, spg)
            nr = lr * sr - li * si + hre[pl.ds(r0, spg), cs]
            ni = lr * si + li * sr + him[pl.ds(r0, spg), cs]
            hre[pl.ds(r0, spg), cs] = nr
            him[pl.ds(r0, spg), cs] = ni
            return nr, ni

        sr, si = lax.fori_loop(0, tc, step, (st_r[:, cs], st_i[:, cs]))
        st_r[:, cs] = sr
        st_i[:, cs] = si

    parts = []
    for m in range(w // MXU_N):
        out = slice(m * MXU_N, (m + 1) * MXU_N)
        src = slice(m * MXU_N * ratio, (m + 1) * MXU_N * ratio)
        parts.append(_dot(hre[:, src].astype(_BF), cr_ref[src, out])
                     - _dot(him[:, src].astype(_BF), ci_ref[src, out]))
    y = jnp.concatenate(parts, axis=1) + d_ref[...] * u
    z = _dot(y.astype(_BF), wg_ref[...]) + bg_ref[...]
    y_ref[...] = y * (1.0 / (1.0 + jnp.exp(-z)))
    hr_out[...] = st_r[...]
    hi_out[...] = st_i[...]


def _s5(u, h0r, h0i, lbr, lbi, bbr, bbi, cr, ci, d, wg, bg, *, tc):
    nbg, rows, w = u.shape
    spg = SEQS_PER_GROUP
    t = rows // spg
    ns = lbr.shape[1]
    ratio = ns // w
    assert ns % MXU_N == 0 and w % MXU_N == 0 and LANES % (MXU_N // ratio) == 0
    const = lambda gi, ti: (0, 0)
    grp = lambda gi, ti: (gi, 0, 0)
    return pl.pallas_call(
        functools.partial(_s5_kernel, tc=tc, cw=min(ns, 4 * LANES)),
        grid=(nbg, t // tc),
        in_specs=[pl.BlockSpec((None, tc * spg, w), lambda gi, ti: (gi, ti, 0)),
                  pl.BlockSpec((None, spg, ns), grp),
                  pl.BlockSpec((None, spg, ns), grp),
                  pl.BlockSpec((1, ns), const), pl.BlockSpec((1, ns), const),
                  pl.BlockSpec((w, ns), const), pl.BlockSpec((w, ns), const),
                  pl.BlockSpec((ns, w), const), pl.BlockSpec((ns, w), const),
                  pl.BlockSpec((1, w), const),
                  pl.BlockSpec((w, w), const), pl.BlockSpec((1, w), const)],
        out_specs=[pl.BlockSpec((None, tc * spg, w), lambda gi, ti: (gi, ti, 0)),
                   pl.BlockSpec((None, spg, ns), grp),
                   pl.BlockSpec((None, spg, ns), grp)],
        out_shape=[jax.ShapeDtypeStruct((nbg, rows, w), _F32),
                   jax.ShapeDtypeStruct((nbg, spg, ns), _F32),
                   jax.ShapeDtypeStruct((nbg, spg, ns), _F32)],
        scratch_shapes=[pltpu.VMEM((tc * spg, ns), _F32), pltpu.VMEM((tc * spg, ns), _F32),
                        pltpu.VMEM((spg, ns), _F32), pltpu.VMEM((spg, ns), _F32)],
        compiler_params=_params(("parallel", "arbitrary")),
        name="s5",
    )(u, h0r, h0i, lbr, lbi, bbr, bbi, cr, ci, d, wg, bg)


def _block_diag(blocks):
    g, r, c = blocks.shape
    eye = jnp.eye(g, dtype=blocks.dtype)
    return (blocks[:, :, None, :] * eye[:, None, :, None]).reshape(g * r, g * c)


def _mix_kernel(x_ref, a_ref, s_ref, woa_ref, wos_ref, gc_ref, wcq_ref, x1_ref, cq_ref):
    x1 = (x_ref[...] + _dot(a_ref[...].astype(_BF), woa_ref[...])
          + _dot(s_ref[...].astype(_BF), wos_ref[...]))
    x1_ref[...] = x1
    cq_ref[...] = _dot(_rms(x1, gc_ref[...]).astype(_BF), wcq_ref[...])


def _mix(x, attn, ssm, ssm_map, woa, wos, gc, wcq, *, nb, nt, tm):
    n, d = x.shape
    row = lambda bi, ti: (bi * nt + ti, 0)
    const = lambda bi, ti: (0, 0)
    mw = wcq.shape[1]
    sw = wos.shape[0]
    return pl.pallas_call(
        _mix_kernel,
        grid=(nb, nt),
        in_specs=[pl.BlockSpec((tm, d), row),
                  pl.BlockSpec((tm, attn.shape[1]), row),
                  pl.BlockSpec((tm, sw), ssm_map),
                  pl.BlockSpec(woa.shape, const), pl.BlockSpec(wos.shape, const),
                  pl.BlockSpec((1, d), const), pl.BlockSpec(wcq.shape, const)],
        out_specs=[pl.BlockSpec((tm, d), row), pl.BlockSpec((tm, mw), row)],
        out_shape=[jax.ShapeDtypeStruct((n, d), _F32), jax.ShapeDtypeStruct((n, mw), _F32)],
        compiler_params=_params(("parallel", "parallel")),
        name="mix",
    )(x, attn, ssm, woa, wos, gc, wcq)


def _mem_kv_kernel(x_ref, g_ref, w_ref, mk_ref, mv_ref):
    hn = _rms(x_ref[...], g_ref[...]).astype(_BF)
    mw = MEM_HEADS * MEM_HEAD_DIM
    for h in range(MEM_HEADS):
        cols = slice(h * MEM_HEAD_DIM, (h + 1) * MEM_HEAD_DIM)
        mk_ref[:, h, :] = _dot(hn, w_ref[:, cols])
        mv_ref[:, h, :] = _dot(hn, w_ref[:, mw + h * MEM_HEAD_DIM:mw + (h + 1) * MEM_HEAD_DIM])


def _mem_kv(mem, g, w, *, tm):
    n, d = mem.shape
    shp = jax.ShapeDtypeStruct((n, MEM_HEADS, MEM_HEAD_DIM), _F32)
    spec = pl.BlockSpec((tm, MEM_HEADS, MEM_HEAD_DIM), lambda i: (i, 0, 0))
    return pl.pallas_call(
        _mem_kv_kernel,
        grid=(n // tm,),
        in_specs=[pl.BlockSpec((tm, d), lambda i: (i, 0)),
                  pl.BlockSpec((1, d), lambda i: (0, 0)),
                  pl.BlockSpec(w.shape, lambda i: (0, 0))],
        out_specs=[spec, spec],
        out_shape=[shp, shp],
        compiler_params=_params(("parallel",)),
        name="mem_kv",
    )(mem, g, w)


def _cross_attn_kernel(q_ref, mk_ref, mv_ref, o_ref, *, sb, rq):
    for s in range(sb):
        rows = slice(s * rq, (s + 1) * rq)
        for h in range(MEM_HEADS):
            cols = slice(h * MEM_HEAD_DIM, (h + 1) * MEM_HEAD_DIM)
            qh = q_ref[rows, cols].astype(_BF)
            sc = _dot_nt(qh, mk_ref[s, :, h, :].astype(_BF)) * (MEM_HEAD_DIM ** -0.5)
            m = jnp.max(sc, axis=-1, keepdims=True)
            e = jnp.exp(sc - m)
            p = e * (1.0 / jnp.sum(e, axis=-1, keepdims=True))
            o_ref[rows, cols] = _dot(p.astype(_BF), mv_ref[s, :, h, :].astype(_BF))


def _cross_attn(cq, mk, mv, *, sb, rq, nt):
    n, mw = cq.shape
    nseq, m = mk.shape[0], mk.shape[1]
    row = lambda si, ti: (si * nt + ti, 0)
    mem = lambda si, ti: (si, 0, 0, 0)
    return pl.pallas_call(
        functools.partial(_cross_attn_kernel, sb=sb, rq=rq),
        grid=(nseq // sb, nt),
        in_specs=[pl.BlockSpec((sb * rq, mw), row),
                  pl.BlockSpec((sb, m, MEM_HEADS, MEM_HEAD_DIM), mem),
                  pl.BlockSpec((sb, m, MEM_HEADS, MEM_HEAD_DIM), mem)],
        out_specs=pl.BlockSpec((sb * rq, mw), row),
        out_shape=jax.ShapeDtypeStruct((n, mw), _F32),
        compiler_params=_params(("parallel", "parallel")),
        name="cross_attn",
    )(cq, mk, mv)


def _mlp_kernel(x1_ref, o_ref, wco_ref, gm_ref, w1_ref, w2_ref, gf_ref, y_ref, *, fc):
    x2 = x1_ref[...] + _dot(o_ref[...].astype(_BF), wco_ref[...])
    hn = _rms(x2, gm_ref[...]).astype(_BF)
    acc = x2
    for c in range(w1_ref.shape[1] // fc):
        cols = slice(c * fc, (c + 1) * fc)
        hid = jnp.square(jnp.maximum(_dot(hn, w1_ref[:, cols]), 0.0)).astype(_BF)
        acc = acc + _dot(hid, w2_ref[cols, :])
    y_ref[...] = _rms(acc, gf_ref[...])


def _mlp(x1, o, wco, gm, w1, w2, gf, *, tm):
    n, d = x1.shape
    row = lambda i: (i, 0)
    const = lambda i: (0, 0)
    return pl.pallas_call(
        functools.partial(_mlp_kernel, fc=min(w1.shape[1], 1024)),
        grid=(n // tm,),
        in_specs=[pl.BlockSpec((tm, d), row), pl.BlockSpec((tm, o.shape[1]), row),
                  pl.BlockSpec(wco.shape, const), pl.BlockSpec((1, d), const),
                  pl.BlockSpec(w1.shape, const), pl.BlockSpec(w2.shape, const),
                  pl.BlockSpec((1, d), const)],
        out_specs=pl.BlockSpec((tm, d), row),
        out_shape=jax.ShapeDtypeStruct((n, d), _F32),
        compiler_params=_params(("parallel",)),
        name="mlp",
    )(x1, o, wco, gm, w1, w2, gf)


def kernel(x_prompt, x_sample, cache_k, cache_v, state_ssm_re, state_ssm_im, cache_mem_k, cache_mem_v,
           page_table, mem_prompt, g_mix, w_in, ssm_a_re, ssm_a_im, ssm_log_dt, ssm_b_re, ssm_b_im,
           ssm_c_re, ssm_c_im, ssm_d, w_glu, b_glu, w_out, g_mem, w_ck, w_cv, g_cross, w_cq, w_co,
           g_mlp, w_ff1, w_ff2, g_final):
    depth = w_in.shape[0]
    assert depth == 1, "single-layer trunk"
    b, t, d = x_prompt.shape
    nseq, dec, _ = x_sample.shape
    a = ATTN_WIDTH
    spg = SEQS_PER_GROUP
    blk = MOBA_BLOCK
    assert t % blk == 0 and b % spg == 0 and nseq % spg == 0 and dec == spg
    n_phys, page = cache_k.shape[1], cache_k.shape[2]
    n_pages = page_table.shape[1]
    past = n_pages * page
    assert past % blk == 0 and blk % page == 0 and dec <= page
    nb = t // blk
    nbp = -(-nb // 8) * 8
    groups, ns_p = ssm_a_re.shape[1], ssm_a_re.shape[2]
    ns = groups * ns_p
    sw = groups * SSM_GROUP
    mtok = mem_prompt.shape[1]
    row2 = lambda z: z.reshape(1, -1)
    hd = (ATTN_HEADS, HEAD_DIM)

    w_in_b = w_in[0].astype(_BF)
    wo_a = w_out[0, :a].astype(_BF)
    wo_s = w_out[0, a:].astype(_BF)
    w_cq_b, w_co_b = w_cq[0].astype(_BF), w_co[0].astype(_BF)
    w1_b, w2_b = w_ff1[0].astype(_BF), w_ff2[0].astype(_BF)
    w_glu_b = w_glu[0].astype(_BF)
    w_ckv_b = jnp.concatenate([w_ck[0], w_cv[0]], axis=1).astype(_BF)
    lbr, lbi, bbr, bbi = _s5_disc(ssm_a_re[0], ssm_a_im[0], ssm_log_dt[0], ssm_b_re[0], ssm_b_im[0])
    lbr, lbi = row2(lbr), row2(lbi)
    bb_r = _block_diag(bbr.reshape(groups, SSM_GROUP, ns_p)).astype(_BF)
    bb_i = _block_diag(bbi.reshape(groups, SSM_GROUP, ns_p)).astype(_BF)
    c_r = _block_diag(jnp.transpose(ssm_c_re[0], (0, 2, 1))).astype(_BF)
    c_i = _block_diag(jnp.transpose(ssm_c_im[0], (0, 2, 1))).astype(_BF)
    s5_w = (lbr, lbi, bb_r, bb_i, c_r, c_i, row2(ssm_d[0]), w_glu_b, row2(b_glu[0]))

    tm = 512 if t % 512 == 0 else blk
    q_p, kt_p, vt_p, kb_p, vtb_p, u_p, km_p = _in_proj_prompt(x_prompt, row2(g_mix[0]), w_in_b, tm=tm)
    km = km_p.reshape(b, nb, ATTN_HEADS, HEAD_DIM)
    km = jnp.pad(jnp.transpose(km, (0, 2, 1, 3)), ((0, 0), (0, 0), (0, nbp - nb), (0, 0)))
    kmblk = (km[:, :, :, None, :] * jnp.eye(ATTN_HEADS, dtype=_F32)[None, :, None, :, None]
             ).reshape(b, ATTN_HEADS * nbp, a)
    attn_p = _moba_prompt(q_p, kb_p.reshape(b, t, a), vtb_p, kmblk)

    zeros_state = jnp.zeros((b // spg, spg, ns), _F32)
    tc = 64 if t % 64 == 0 else t
    ssm_p, hr_p, hi_p = _s5(u_p.reshape(b // spg, t * spg, sw), zeros_state, zeros_state, *s5_w, tc=tc)

    nt = t // tm
    ssm_map_p = lambda bi, ti: (bi // spg * nt + ti, bi % spg)
    x1_p, cq_p = _mix(x_prompt.reshape(b * t, d), attn_p, ssm_p.reshape(b // spg * t, spg * sw), ssm_map_p,
                      wo_a, wo_s, row2(g_cross[0]), w_cq_b, nb=b, nt=nt, tm=tm)
    mk_p, mv_p = _mem_kv(mem_prompt.reshape(b * mtok, d), row2(g_mem[0]), w_ckv_b, tm=mtok)
    mem4 = (b, mtok, MEM_HEADS, MEM_HEAD_DIM)
    o_p = _cross_attn(cq_p, mk_p.reshape(mem4), mv_p.reshape(mem4), sb=1, rq=tm, nt=nt)
    y_p = _mlp(x1_p, o_p, w_co_b, row2(g_mlp[0]), w1_b, w2_b, row2(g_final), tm=tm)

    n_s = nseq * dec
    tms = 256 if n_s % 256 == 0 else n_s
    q_s, k_s, v_s, u_s = _norm_proj(x_sample.reshape(n_s, d), row2(g_mix[0]), w_in_b, (a, a, a, sw),
                                    tm=tms, name="in_proj_sample")
    nq = ATTN_HEADS * dec
    head_of_row = jnp.arange(nq) // dec
    head_of_col = jnp.arange(a) // HEAD_DIM
    head_mask = (head_of_row[:, None] == head_of_col[None, :]).astype(_F32)
    wq = jnp.tile(q_s.reshape(nseq, dec, a), (1, ATTN_HEADS, 1)) * head_mask[None]
    pad_new = lambda z: jnp.pad(z.reshape(nseq, dec, a), ((0, 0), (0, page - dec), (0, 0))).astype(_BF)
    slope_rows = jnp.asarray(_slopes(), _F32)[head_of_row].reshape(nq, 1)
    qpos_rows = (past + jnp.arange(nq) % dec).astype(_F32).reshape(nq, 1)
    token_minor = lambda c: jnp.transpose(c[0], (0, 2, 3, 1)).reshape(n_phys, a, page)
    pg = 16 if n_pages % 16 == 0 else blk // page
    attn_s = _moba_sample(page_table, wq, pad_new(k_s), pad_new(v_s), slope_rows, qpos_rows,
                          token_minor(cache_k), token_minor(cache_v), pg=pg, past=past)

    to_groups = lambda z: jnp.transpose(z.reshape(nseq // spg, spg, dec, -1), (0, 2, 1, 3))
    u_g = to_groups(u_s).reshape(nseq // spg, dec * spg, sw)
    h0r = state_ssm_re[0].reshape(nseq // spg, spg, ns)
    h0i = state_ssm_im[0].reshape(nseq // spg, spg, ns)
    ssm_g, hr_s, hi_s = _s5(u_g, h0r, h0i, *s5_w, tc=dec)
    ssm_s = jnp.transpose(ssm_g.reshape(nseq // spg, dec, spg, sw), (0, 2, 1, 3)).reshape(n_s, sw)

    nts = n_s // tms
    x1_s, cq_s = _mix(x_sample.reshape(n_s, d), attn_s.reshape(n_s, a), ssm_s, lambda bi, ti: (ti, 0),
                      wo_a, wo_s, row2(g_cross[0]), w_cq_b, nb=1, nt=nts, tm=tms)
    o_s = _cross_attn(cq_s, cache_mem_k[0], cache_mem_v[0], sb=spg, rq=dec, nt=1)
    y_s = _mlp(x1_s, o_s, w_co_b, row2(g_mlp[0]), w1_b, w2_b, row2(g_final), tm=tms)

    kv_out = lambda z: jnp.transpose(z.reshape((1, b) + hd + (t,)), (0, 1, 4, 2, 3))
    return (y_p.reshape(b, t, d), y_s.reshape(nseq, dec, d),
            kv_out(kt_p), kv_out(vt_p),
            hr_p.reshape(1, b, groups, ns_p), hi_p.reshape(1, b, groups, ns_p),
            mk_p.reshape((1,) + mem4), mv_p.reshape((1,) + mem4),
            k_s.reshape((1, nseq, dec) + hd), v_s.reshape((1, nseq, dec) + hd),
            hr_s.reshape(1, nseq, groups, ns_p), hi_s.reshape(1, nseq, groups, ns_p))
```

```python
import functools

import jax
import jax.numpy as jnp
from jax import lax
from jax.experimental import pallas as pl
from jax.experimental.pallas import tpu as pltpu

ATTN_HEADS = 8
HEAD_DIM = 64
ATTN_WIDTH = ATTN_HEADS * HEAD_DIM
MOBA_BLOCK = 256
MOBA_TOPK = 3
SSM_GROUP = 16
MEM_HEADS = 4
MEM_HEAD_DIM = 128
RMS_EPS = 1e-6
SEQS_PER_GROUP = 8
LANES = 128
MXU_N = 256
VMEM_LIMIT = 56 * 1024 * 1024

_BF = jnp.bfloat16
_F32 = jnp.float32
_NEG = -1e30
_BIG = 1e30
_LOG2E = 1.4426950408889634
_NT = (((1,), (1,)), ((), ()))


def _slopes():
    return [2.0 ** (-8.0 * (h + 1) / ATTN_HEADS) for h in range(ATTN_HEADS)]


def _rms(x, g):
    return x * lax.rsqrt(jnp.mean(x * x, axis=-1, keepdims=True) + RMS_EPS) * g


def _dot(a, b):
    return jnp.dot(a, b, preferred_element_type=_F32)


def _dot_nt(a, b):
    return lax.dot_general(a, b, _NT, preferred_element_type=_F32)


def _split_bf16(x):
    hi = x.astype(_BF)
    lo = (x - hi.astype(_F32)).astype(_BF)
    return hi, lo


def _dot_nt_3pass(a, b):
    a_hi, a_lo = _split_bf16(a)
    b_hi, b_lo = _split_bf16(b)
    return _dot_nt(a_hi, b_hi) + _dot_nt(a_hi, b_lo) + _dot_nt(a_lo, b_hi)


def _rank_update(rank, gm, g, tie):
    return rank + jnp.where(gm > g, 1.0, jnp.where(gm == g, tie, 0.0))


def _params(sem):
    return pltpu.CompilerParams(dimension_semantics=sem, vmem_limit_bytes=VMEM_LIMIT)


def _in_proj_prompt_kernel(x_ref, g_ref, w_ref, q_ref, kt_ref, vt_ref, kb_ref, vtb_ref, u_ref, km_ref):
    a = ATTN_WIDTH
    hn = _rms(x_ref[...], g_ref[...]).astype(_BF)
    q_ref[...] = _dot(hn, w_ref[:, 0:a])
    k = _dot(hn, w_ref[:, a:2 * a])
    v = _dot(hn, w_ref[:, 2 * a:3 * a])
    kt_ref[...] = k.T
    vt = v.T
    vt_ref[...] = vt
    kb_ref[...] = k.astype(_BF)
    u_ref[...] = _dot(hn, w_ref[:, 3 * a:])
    for r in range(k.shape[0] // MOBA_BLOCK):
        rows = slice(r * MOBA_BLOCK, (r + 1) * MOBA_BLOCK)
        vtb_ref[r] = vt[:, rows].astype(_BF)
        km_ref[r] = jnp.mean(k[rows, :], axis=0, keepdims=True)


def _in_proj_prompt(x, g, w, *, tm):
    b, t, d = x.shape
    a = ATTN_WIDTH
    su = w.shape[1] - 3 * a
    nt = t // tm
    nbt = tm // MOBA_BLOCK
    nbg = b // SEQS_PER_GROUP
    row = lambda bi, ti: (bi * nt + ti, 0)
    tok_minor = lambda bi, ti: (bi, 0, ti)
    return pl.pallas_call(
        _in_proj_prompt_kernel,
        grid=(b, nt),
        in_specs=[pl.BlockSpec((tm, d), row),
                  pl.BlockSpec((1, d), lambda bi, ti: (0, 0)),
                  pl.BlockSpec(w.shape, lambda bi, ti: (0, 0))],
        out_specs=[pl.BlockSpec((tm, a), row),
                   pl.BlockSpec((None, a, tm), tok_minor),
                   pl.BlockSpec((None, a, tm), tok_minor),
                   pl.BlockSpec((tm, a), row),
                   pl.BlockSpec((None, nbt, a, MOBA_BLOCK), lambda bi, ti: (bi, ti, 0, 0)),
                   pl.BlockSpec((None, tm, su), lambda bi, ti: (bi // SEQS_PER_GROUP, ti, bi % SEQS_PER_GROUP)),
                   pl.BlockSpec((nbt, 1, a), lambda bi, ti: (bi * nt + ti, 0, 0))],
        out_shape=[jax.ShapeDtypeStruct((b * t, a), _F32),
                   jax.ShapeDtypeStruct((b, a, t), _F32),
                   jax.ShapeDtypeStruct((b, a, t), _F32),
                   jax.ShapeDtypeStruct((b * t, a), _BF),
                   jax.ShapeDtypeStruct((b, t // MOBA_BLOCK, a, MOBA_BLOCK), _BF),
                   jax.ShapeDtypeStruct((nbg, t, SEQS_PER_GROUP * su), _F32),
                   jax.ShapeDtypeStruct((b * t // MOBA_BLOCK, 1, a), _F32)],
        compiler_params=_params(("parallel", "parallel")),
        name="in_proj_prompt",
    )(x.reshape(b * t, d), g, w)


def _norm_proj_kernel(x_ref, g_ref, w_ref, *o_refs):
    hn = _rms(x_ref[...], g_ref[...]).astype(_BF)
    c0 = 0
    for o_ref in o_refs:
        c1 = c0 + o_ref.shape[1]
        o_ref[...] = _dot(hn, w_ref[:, c0:c1])
        c0 = c1


def _norm_proj(x, g, w, widths, *, tm, name):
    n, d = x.shape
    row = lambda i: (i, 0)
    return pl.pallas_call(
        _norm_proj_kernel,
        grid=(n // tm,),
        in_specs=[pl.BlockSpec((tm, d), row),
                  pl.BlockSpec((1, d), lambda i: (0, 0)),
                  pl.BlockSpec(w.shape, lambda i: (0, 0))],
        out_specs=[pl.BlockSpec((tm, c), row) for c in widths],
        out_shape=[jax.ShapeDtypeStruct((n, c), _F32) for c in widths],
        compiler_params=_params(("parallel",)),
        name=name,
    )(x, g, w)


def _moba_prompt_kernel(q_ref, k_ref, vt_ref, km_ref, o_ref,
                        sel_ref, bias_ref, qx_ref, sa_ref, sb_ref, m_ref, l_ref, acc_ref, *, nb, nbp):
    blk = MOBA_BLOCK
    nh = ATTN_HEADS
    i = pl.program_id(1)
    q = q_ref[...]
    slopes2 = [s * _LOG2E for s in _slopes()]

    gt = _dot_nt_3pass(km_ref[...], q)
    rows = lax.broadcasted_iota(jnp.int32, (nbp, blk), 0)
    fully_past = rows < i
    g = [jnp.where(fully_past, gt[h * nbp:(h + 1) * nbp, :], _NEG) for h in range(nh)]
    rank = [jnp.zeros((nbp, blk), _F32) for _ in range(nh)]
    for m in range(nb):
        tie = jnp.where(rows > m, 1.0, 0.0)
        for h in range(nh):
            rank[h] = _rank_update(rank[h], g[h][m:m + 1, :], g[h], tie)
    for h in range(nh):
        sel_ref[h * nbp:(h + 1) * nbp, :] = jnp.where(fully_past & (rank[h] < MOBA_TOPK), 1.0, 0.0)

    qs = (q * (HEAD_DIM ** -0.5 * _LOG2E)).astype(_BF)
    lane = lax.broadcasted_iota(jnp.int32, (blk, LANES), 1)
    zero = jnp.zeros((), _BF)
    for p in range(nh // 2):
        qp = qs[:, p * LANES:(p + 1) * LANES]
        qx_ref[p, 0:blk, :] = jnp.where(lane < HEAD_DIM, qp, zero)
        qx_ref[p, blk:2 * blk, :] = jnp.where(lane >= HEAD_DIM, qp, zero)

    kk = lax.broadcasted_iota(jnp.int32, (blk, blk), 0)
    qq = lax.broadcasted_iota(jnp.int32, (blk, blk), 1)
    dkq = (kk - qq).astype(_F32)
    @pl.when(i == 0)
    def _fill_bias():
        for h in range(nh):
            bias_ref[h] = slopes2[h] * dkq

    causal_neg = jnp.where(dkq <= 0.0, 0.0, _NEG)
    ones_rows = jnp.ones((8, blk), _BF)

    def scores(j, s_out, p):
        start = pl.multiple_of(j * blk, blk)
        kb = k_ref[pl.ds(start, blk), p * LANES:(p + 1) * LANES]
        s_out[p] = _dot_nt(kb, qx_ref[p])

    def step(j_next, s_next, j, s_cur, own):
        coff = None if own else ((i - j) * blk).astype(_F32)
        for p in range(nh // 2):
            if j_next is not None:
                scores(j_next, s_next, p)
            vb = vt_ref[j, p * LANES:(p + 1) * LANES, :]
            ptbs = []
            alphas = []
            for x in range(2):
                h = 2 * p + x
                s = s_cur[p, :, x * blk:(x + 1) * blk] + bias_ref[h]
                if own:
                    s = s + causal_neg
                    m_new = jnp.max(s, axis=0, keepdims=True)
                    t = m_new
                    alphas.append(None)
                else:
                    c = slopes2[h] * coff
                    picked = sel_ref[pl.ds(h * nbp + j, 1), :] > 0.5
                    m_old = m_ref[h]
                    m_new = jnp.where(picked, jnp.maximum(m_old, jnp.max(s, axis=0, keepdims=True) - c), m_old)
                    t = jnp.where(picked, m_new + c, _BIG)
                    alphas.append(jnp.exp2(m_old - m_new))
                m_ref[h] = m_new
                ptbs.append(jnp.exp2(s - t).astype(_BF))
            for x in range(2):
                h = 2 * p + x
                arow = slice(h * HEAD_DIM, (h + 1) * HEAD_DIM)
                psum = _dot(ones_rows, ptbs[x])[0:1, :]
                pv = _dot(vb[x * HEAD_DIM:(x + 1) * HEAD_DIM, :], ptbs[x])
                if own:
                    l_ref[h] = psum
                    acc_ref[arow, :] = pv
                else:
                    l_ref[h] = alphas[x] * l_ref[h] + psum
                    acc_ref[arow, :] = acc_ref[arow, :] * alphas[x] + pv

    for p in range(nh // 2):
        scores(i, sa_ref, p)
    step(0, sb_ref, i, sa_ref, True)
    last = jnp.maximum(i - 1, 0)

    def pair(mi, carry):
        b0 = 2 * mi
        step(b0 + 1, sa_ref, b0, sb_ref, False)
        step(jnp.minimum(b0 + 2, last), sb_ref, b0 + 1, sa_ref, False)
        return carry

    lax.fori_loop(0, i // 2, pair, 0)

    @pl.when(i % 2 == 1)
    def _odd_tail():
        step(None, None, i - 1, sb_ref, False)

    for h in range(nh):
        arow = slice(h * HEAD_DIM, (h + 1) * HEAD_DIM)
        acc_ref[arow, :] = acc_ref[arow, :] * (1.0 / l_ref[h])
    o_ref[...] = acc_ref[...].T


def _moba_prompt(q, kb, vt, kmblk):
    b, t, a = kb.shape
    blk = MOBA_BLOCK
    nb = t // blk
    nbp = kmblk.shape[1] // ATTN_HEADS
    return pl.pallas_call(
        functools.partial(_moba_prompt_kernel, nb=nb, nbp=nbp),
        grid=(b, nb),
        in_specs=[pl.BlockSpec((blk, a), lambda bi, i: (bi * nb + i, 0)),
                  pl.BlockSpec((None, t, a), lambda bi, i: (bi, 0, 0)),
                  pl.BlockSpec((None, nb, a, blk), lambda bi, i: (bi, 0, 0, 0)),
                  pl.BlockSpec((None,) + kmblk.shape[1:], lambda bi, i: (bi, 0, 0))],
        out_specs=pl.BlockSpec((blk, a), lambda bi, i: (bi * nb + i, 0)),
        out_shape=jax.ShapeDtypeStruct((b * t, a), _F32),
        scratch_shapes=[pltpu.VMEM((ATTN_HEADS * nbp, blk), _F32),
                        pltpu.VMEM((ATTN_HEADS, blk, blk), _F32),
                        pltpu.VMEM((ATTN_HEADS // 2, 2 * blk, LANES), _BF),
                        pltpu.VMEM((ATTN_HEADS // 2, blk, 2 * blk), _F32),
                        pltpu.VMEM((ATTN_HEADS // 2, blk, 2 * blk), _F32),
                        pltpu.VMEM((ATTN_HEADS, 1, blk), _F32),
                        pltpu.VMEM((ATTN_HEADS, 1, blk), _F32),
                        pltpu.VMEM((a, blk), _F32)],
        compiler_params=_params(("parallel", "arbitrary")),
        name="moba_prompt",
    )(q, kb, vt, kmblk)


def _moba_sample_kernel(pt_ref, wq_ref, kn_ref, vn_ref, slope_ref, qpos_ref, *rest,
                        pg, n_pages, page, past, dec):
    del pt_ref
    k_refs = rest[:pg]
    v_refs = rest[pg:2 * pg]
    o_ref, s_ref, gate_ref, acc_ref, l_ref = rest[2 * pg:]
    blk = MOBA_BLOCK
    ppb = blk // page
    nb = n_pages // ppb
    n_groups = n_pages // pg
    step = pl.program_id(1)
    wq = wq_ref[...]
    nq = wq.shape[0]
    wqb = (wq * (HEAD_DIM ** -0.5 * _LOG2E)).astype(_BF)
    blk_lane = lax.broadcasted_iota(jnp.int32, (nq, LANES), 1)

    @pl.when(step == 0)
    def _init():
        gate_ref[...] = jnp.zeros(gate_ref.shape, _F32)

    @pl.when(step < n_groups)
    def _k_phase():
        for c in range(pg // ppb):
            gs = jnp.zeros((nq, page), _F32)
            for r in range(c * ppb, (c + 1) * ppb):
                st = _dot(wqb, k_refs[r][...].astype(_BF))
                s_ref[step * pg + r] = st
                gs = gs + st
            n = step * (pg // ppb) + c
            gsum = jnp.sum(gs, axis=1, keepdims=True)
            gate_ref[...] = jnp.where(blk_lane == n, gsum, gate_ref[...])

    @pl.when(step == n_groups - 1)
    def _softmax():
        gate = gate_ref[...]
        rank = jnp.zeros(gate.shape, _F32)
        for m in range(nb):
            tie = jnp.where(blk_lane > m, 1.0, 0.0)
            rank = _rank_update(rank, gate[:, m:m + 1], gate, tie)
        drop = jnp.where(rank < MOBA_TOPK, 0.0, _NEG)
        lane = lax.broadcasted_iota(jnp.int32, (nq, page), 1).astype(_F32)
        slope2 = jnp.broadcast_to(slope_ref[...] * _LOG2E, (nq, page))
        qpos = jnp.broadcast_to(qpos_ref[...], (nq, page))
        base = slope2 * (lane - qpos)

        mm = jnp.full((nq, page), _NEG, _F32)
        for p in range(n_pages):
            n = p // ppb
            z = s_ref[p] + (base + slope2 * float(p * page)) + drop[:, n:n + 1]
            s_ref[p] = z
            mm = jnp.maximum(mm, z)
        so = _dot_nt(wqb, kn_ref[...]) + (base + slope2 * float(past))
        so = jnp.where(lane + float(past) <= qpos, so, _NEG)
        m = jnp.max(jnp.maximum(mm, so), axis=1, keepdims=True)
        ls = jnp.exp2(so - m)
        acc_ref[...] = _dot(ls.astype(_BF), vn_ref[...])
        for p in range(n_pages):
            pp = jnp.exp2(s_ref[p] - m)
            s_ref[p] = pp
            ls = ls + pp
        l_ref[...] = jnp.sum(ls, axis=1, keepdims=True)

    @pl.when(step >= n_groups)
    def _v_phase():
        acc = acc_ref[...]
        for r in range(pg):
            pp = s_ref[(step - n_groups) * pg + r]
            acc = acc + _dot_nt(pp.astype(_BF), v_refs[r][...].astype(_BF))
        acc_ref[...] = acc

    @pl.when(step == 2 * n_groups - 1)
    def _finish():
        o = acc_ref[...] * (1.0 / l_ref[...])
        lane_a = lax.broadcasted_iota(jnp.int32, (dec, o.shape[1]), 1)
        out = jnp.zeros((dec, o.shape[1]), _F32)
        for h in range(ATTN_HEADS):
            in_head = (lane_a >= h * HEAD_DIM) & (lane_a < (h + 1) * HEAD_DIM)
            out = out + jnp.where(in_head, o[h * dec:(h + 1) * dec, :], 0.0)
        o_ref[...] = out


def _moba_sample(page_table, wq, kn, vn, slope_rows, qpos_rows, cache_kt, cache_vt, *, pg, past):
    nseq, n_pages = page_table.shape
    _, a, page = cache_kt.shape
    nq = wq.shape[1]
    dec = nq // ATTN_HEADS
    n_groups = n_pages // pg
    assert n_pages * page // MOBA_BLOCK <= LANES
    seq = lambda bi, s, pt: (bi, 0, 0)
    const = lambda bi, s, pt: (0, 0)

    def k_map(r):
        return lambda bi, s, pt: (pt[bi, jnp.minimum(s, n_groups - 1) * pg + r], 0, 0)

    def v_map(r):
        def index(bi, s, pt):
            in_v = s >= n_groups
            seq_i = jnp.where(in_v, bi, jnp.maximum(bi - 1, 0))
            grp = jnp.where(in_v, s - n_groups, n_groups - 1)
            return (pt[seq_i, grp * pg + r], 0, 0)
        return index

    in_specs = [pl.BlockSpec((None, nq, a), seq),
                pl.BlockSpec((None, page, a), seq),
                pl.BlockSpec((None, page, a), seq),
                pl.BlockSpec((nq, 1), const),
                pl.BlockSpec((nq, 1), const)]
    in_specs += [pl.BlockSpec((None, a, page), k_map(r)) for r in range(pg)]
    in_specs += [pl.BlockSpec((None, a, page), v_map(r)) for r in range(pg)]
    return pl.pallas_call(
        functools.partial(_moba_sample_kernel, pg=pg, n_pages=n_pages, page=page, past=past, dec=dec),
        grid_spec=pltpu.PrefetchScalarGridSpec(
            num_scalar_prefetch=1,
            grid=(nseq, 2 * n_groups),
            in_specs=in_specs,
            out_specs=pl.BlockSpec((None, dec, a), seq),
            scratch_shapes=[pltpu.VMEM((n_pages, nq, page), _F32),
                            pltpu.VMEM((nq, LANES), _F32),
                            pltpu.VMEM((nq, a), _F32),
                            pltpu.VMEM((nq, 1), _F32)]),
        out_shape=jax.ShapeDtypeStruct((nseq, dec, a), _F32),
        compiler_params=_params(("parallel", "arbitrary")),
        name="moba_sample",
    )(page_table, wq, kn, vn, slope_rows, qpos_rows, *([cache_kt] * pg), *([cache_vt] * pg))


def _s5_disc_kernel(are_ref, aim_ref, ldt_ref, bre_ref, bim_ref, lbr_ref, lbi_ref, bbr_ref, bbi_ref):
    are = are_ref[...]
    aim = aim_ref[...]
    dt = jnp.exp(ldt_ref[...])
    mag = jnp.exp(are * dt)
    ang = aim * dt
    lb_re = mag * jnp.cos(ang)
    lb_im = mag * jnp.sin(ang)
    e_re = lb_re - 1.0
    e_im = lb_im
    den = are * are + aim * aim
    f_re = (e_re * are + e_im * aim) / den
    f_im = (e_im * are - e_re * aim) / den
    br = bre_ref[...]
    bi = bim_ref[...]
    lbr_ref[...] = lb_re
    lbi_ref[...] = lb_im
    bbr_ref[...] = f_re * br - f_im * bi
    bbi_ref[...] = f_re * bi + f_im * br


def _s5_disc(a_re, a_im, log_dt, b_re, b_im):
    g, p, c = b_re.shape
    rep = lambda z: jnp.repeat(z, c, axis=0)
    are, aim = rep(a_re), rep(a_im)
    ldt = rep(log_dt.reshape(g, 1))
    b2 = lambda z: jnp.transpose(z, (0, 2, 1)).reshape(g * c, p)
    shp = jax.ShapeDtypeStruct((g * c, p), _F32)
    lbr, lbi, bbr, bbi = pl.pallas_call(
        _s5_disc_kernel, out_shape=[shp] * 4, name="s5_disc",
    )(are, aim, ldt, b2(b_re), b2(b_im))
    return lbr[::c], lbi[::c], bbr, bbi


def _s5_kernel(u_ref, h0r_ref, h0i_ref, lbr_ref, lbi_ref, bbr_ref, bbi_ref, cr_ref, ci_ref, d_ref,
               wg_ref, bg_ref, y_ref, hr_out, hi_out, hre, him, st_r, st_i, *, tc, cw):
    spg = SEQS_PER_GROUP
    t = pl.program_id(1)
    u = u_ref[...]
    ub = u.astype(_BF)
    w = u.shape[1]
    ns = hre.shape[1]
    ratio = ns // w
    for n in range(ns // MXU_N):
        out = slice(n * MXU_N, (n + 1) * MXU_N)
        lo = (n * MXU_N // ratio) // LANES * LANES
        src = slice(lo, lo + LANES)
        hre[:, out] = _dot(ub[:, src], bbr_ref[src, out])
        him[:, out] = _dot(ub[:, src], bbi_ref[src, out])

    @pl.when(t == 0)
    def _init():
        st_r[...] = h0r_ref[...]
        st_i[...] = h0i_ref[...]

    for c in range(ns // cw):
        cs = slice(c * cw, (c + 1) * cw)
        lr = jnp.broadcast_to(lbr_ref[:, cs], (spg, cw))
        li = jnp.broadcast_to(lbi_ref[:, cs], (spg, cw))

        def step(tt, carry):
            sr, si = carry
            r0 = pl.multiple_of(tt * spg, spg)
            nr = lr * sr - li * si + hre[pl.ds(r0, spg), cs]
            ni = lr * si + li * sr + him[pl.ds(r0, spg), cs]
            hre[pl.ds(r0, spg), cs] = nr
            him[pl.ds(r0, spg), cs] = ni
            return nr, ni

        sr, si = lax.fori_loop(0, tc, step, (st_r[:, cs], st_i[:, cs]))
        st_r[:, cs] = sr
        st_i[:, cs] = si

    parts = []
    for m in range(w // MXU_N):
        out = slice(m * MXU_N, (m + 1) * MXU_N)
        src = slice(m * MXU_N * ratio, (m + 1) * MXU_N * ratio)
        parts.append(_dot(hre[:, src].astype(_BF), cr_ref[src, out])
                     - _dot(him[:, src].astype(_BF), ci_ref[src, out]))
    y = jnp.concatenate(parts, axis=1) + d_ref[...] * u
    z = _dot(y.astype(_BF), wg_ref[...]) + bg_ref[...]
    y_ref[...] = y * (1.0 / (1.0 + jnp.exp(-z)))
    hr_out[...] = st_r[...]
    hi_out[...] = st_i[...]


def _s5(u, h0r, h0i, lbr, lbi, bbr, bbi, cr, ci, d, wg, bg, *, tc):
    nbg, rows, w = u.shape
    spg = SEQS_PER_GROUP
    t = rows // spg
    ns = lbr.shape[1]
    ratio = ns // w
    assert ns % MXU_N == 0 and w % MXU_N == 0 and LANES % (MXU_N // ratio) == 0
    const = lambda gi, ti: (0, 0)
    grp = lambda gi, ti: (gi, 0, 0)
    return pl.pallas_call(
        functools.partial(_s5_kernel, tc=tc, cw=min(ns, 4 * LANES)),
        grid=(nbg, t // tc),
        in_specs=[pl.BlockSpec((None, tc * spg, w), lambda gi, ti: (gi, ti, 0)),
                  pl.BlockSpec((None, spg, ns), grp),
                  pl.BlockSpec((None, spg, ns), grp),
                  pl.BlockSpec((1, ns), const), pl.BlockSpec((1, ns), const),
                  pl.BlockSpec((w, ns), const), pl.BlockSpec((w, ns), const),
                  pl.BlockSpec((ns, w), const), pl.BlockSpec((ns, w), const),
                  pl.BlockSpec((1, w), const),
                  pl.BlockSpec((w, w), const), pl.BlockSpec((1, w), const)],
        out_specs=[pl.BlockSpec((None, tc * spg, w), lambda gi, ti: (gi, ti, 0)),
                   pl.BlockSpec((None, spg, ns), grp),
                   pl.BlockSpec((None, spg, ns), grp)],
        out_shape=[jax.ShapeDtypeStruct((nbg, rows, w), _F32),
                   jax.ShapeDtypeStruct((nbg, spg, ns), _F32),
                   jax.ShapeDtypeStruct((nbg, spg, ns), _F32)],
        scratch_shapes=[pltpu.VMEM((tc * spg, ns), _F32), pltpu.VMEM((tc * spg, ns), _F32),
                        pltpu.VMEM((spg, ns), _F32), pltpu.VMEM((spg, ns), _F32)],
        compiler_params=_params(("parallel", "arbitrary")),
        name="s5",
    )(u, h0r, h0i, lbr, lbi, bbr, bbi, cr, ci, d, wg, bg)


def _block_diag(blocks):
    g, r, c = blocks.shape
    eye = jnp.eye(g, dtype=blocks.dtype)
    return (blocks[:, :, None, :] * eye[:, None, :, None]).reshape(g * r, g * c)


def _mix_kernel(x_ref, a_ref, s_ref, woa_ref, wos_ref, gc_ref, wcq_ref, x1_ref, cq_ref):
    x1 = (x_ref[...] + _dot(a_ref[...].astype(_BF), woa_ref[...])
          + _dot(s_ref[...].astype(_BF), wos_ref[...]))
    x1_ref[...] = x1
    cq_ref[...] = _dot(_rms(x1, gc_ref[...]).astype(_BF), wcq_ref[...])


def _mix(x, attn, ssm, ssm_map, woa, wos, gc, wcq, *, nb, nt, tm):
    n, d = x.shape
    row = lambda bi, ti: (bi * nt + ti, 0)
    const = lambda bi, ti: (0, 0)
    mw = wcq.shape[1]
    sw = wos.shape[0]
    return pl.pallas_call(
        _mix_kernel,
        grid=(nb, nt),
        in_specs=[pl.BlockSpec((tm, d), row),
                  pl.BlockSpec((tm, attn.shape[1]), row),
                  pl.BlockSpec((tm, sw), ssm_map),
                  pl.BlockSpec(woa.shape, const), pl.BlockSpec(wos.shape, const),
                  pl.BlockSpec((1, d), const), pl.BlockSpec(wcq.shape, const)],
        out_specs=[pl.BlockSpec((tm, d), row), pl.BlockSpec((tm, mw), row)],
        out_shape=[jax.ShapeDtypeStruct((n, d), _F32), jax.ShapeDtypeStruct((n, mw), _F32)],
        compiler_params=_params(("parallel", "parallel")),
        name="mix",
    )(x, attn, ssm, woa, wos, gc, wcq)


def _mem_kv_kernel(x_ref, g_ref, w_ref, mk_ref, mv_ref, mkb_ref, mvb_ref):
    hn = _rms(x_ref[...], g_ref[...]).astype(_BF)
    mw = MEM_HEADS * MEM_HEAD_DIM
    k = _dot(hn, w_ref[:, :mw])
    v = _dot(hn, w_ref[:, mw:])
    mkb_ref[...] = k.astype(_BF)
    mvb_ref[...] = v.astype(_BF)
    for h in range(MEM_HEADS):
        cols = slice(h * MEM_HEAD_DIM, (h + 1) * MEM_HEAD_DIM)
        mk_ref[:, h, :] = k[:, cols]
        mv_ref[:, h, :] = v[:, cols]


def _mem_kv(mem, g, w, *, tm):
    n, d = mem.shape
    mw = MEM_HEADS * MEM_HEAD_DIM
    shp = jax.ShapeDtypeStruct((n, MEM_HEADS, MEM_HEAD_DIM), _F32)
    spec = pl.BlockSpec((tm, MEM_HEADS, MEM_HEAD_DIM), lambda i: (i, 0, 0))
    flat = pl.BlockSpec((tm, mw), lambda i: (i, 0))
    return pl.pallas_call(
        _mem_kv_kernel,
        grid=(n // tm,),
        in_specs=[pl.BlockSpec((tm, d), lambda i: (i, 0)),
                  pl.BlockSpec((1, d), lambda i: (0, 0)),
                  pl.BlockSpec(w.shape, lambda i: (0, 0))],
        out_specs=[spec, spec, flat, flat],
        out_shape=[shp, shp, jax.ShapeDtypeStruct((n, mw), _BF), jax.ShapeDtypeStruct((n, mw), _BF)],
        compiler_params=_params(("parallel",)),
        name="mem_kv",
    )(mem, g, w)


def _cross_attn_kernel(q_ref, mk_ref, mv_ref, o_ref, *, sb, rq):
    for s in range(sb):
        rows = slice(s * rq, (s + 1) * rq)
        for h in range(MEM_HEADS):
            cols = slice(h * MEM_HEAD_DIM, (h + 1) * MEM_HEAD_DIM)
            qh = q_ref[rows, cols].astype(_BF)
            sc = _dot_nt(qh, mk_ref[s, :, h, :].astype(_BF)) * (MEM_HEAD_DIM ** -0.5)
            m = jnp.max(sc, axis=-1, keepdims=True)
            e = jnp.exp(sc - m)
            p = e * (1.0 / jnp.sum(e, axis=-1, keepdims=True))
            o_ref[rows, cols] = _dot(p.astype(_BF), mv_ref[s, :, h, :].astype(_BF))


def _cross_attn(cq, mk, mv, *, sb, rq, nt):
    n, mw = cq.shape
    nseq, m = mk.shape[0], mk.shape[1]
    row = lambda si, ti: (si * nt + ti, 0)
    mem = lambda si, ti: (si, 0, 0, 0)
    return pl.pallas_call(
        functools.partial(_cross_attn_kernel, sb=sb, rq=rq),
        grid=(nseq // sb, nt),
        in_specs=[pl.BlockSpec((sb * rq, mw), row),
                  pl.BlockSpec((sb, m, MEM_HEADS, MEM_HEAD_DIM), mem),
                  pl.BlockSpec((sb, m, MEM_HEADS, MEM_HEAD_DIM), mem)],
        out_specs=pl.BlockSpec((sb * rq, mw), row),
        out_shape=jax.ShapeDtypeStruct((n, mw), _F32),
        compiler_params=_params(("parallel", "parallel")),
        name="cross_attn",
    )(cq, mk, mv)


def _mlp_tail(x2, gm_ref, w1_ref, w2_ref, gf_ref, fc):
    hn = _rms(x2, gm_ref[...]).astype(_BF)
    acc = x2
    for c in range(w1_ref.shape[1] // fc):
        cols = slice(c * fc, (c + 1) * fc)
        hid = jnp.square(jnp.maximum(_dot(hn, w1_ref[:, cols]), 0.0)).astype(_BF)
        acc = acc + _dot(hid, w2_ref[cols, :])
    return _rms(acc, gf_ref[...])


def _mlp_kernel(x1_ref, o_ref, wco_ref, gm_ref, w1_ref, w2_ref, gf_ref, y_ref, *, fc):
    x2 = x1_ref[...] + _dot(o_ref[...].astype(_BF), wco_ref[...])
    y_ref[...] = _mlp_tail(x2, gm_ref, w1_ref, w2_ref, gf_ref, fc)


def _post_kernel(x_ref, a_ref, s_ref, mk_ref, mv_ref, woa_ref, wos_ref, gc_ref, wcq_ref, wco_ref,
                 gm_ref, w1_ref, w2_ref, gf_ref, y_ref, *, fc):
    x1 = (x_ref[...] + _dot(a_ref[...].astype(_BF), woa_ref[...])
          + _dot(s_ref[...].astype(_BF), wos_ref[...]))
    cq = _dot(_rms(x1, gc_ref[...]).astype(_BF), wcq_ref[...]).astype(_BF)
    heads = []
    for h in range(MEM_HEADS):
        cols = slice(h * MEM_HEAD_DIM, (h + 1) * MEM_HEAD_DIM)
        sc = _dot_nt(cq[:, cols], mk_ref[:, cols]) * (MEM_HEAD_DIM ** -0.5)
        m = jnp.max(sc, axis=-1, keepdims=True)
        e = jnp.exp(sc - m)
        p = e * (1.0 / jnp.sum(e, axis=-1, keepdims=True))
        heads.append(_dot(p.astype(_BF), mv_ref[:, cols]))
    o = jnp.concatenate(heads, axis=1)
    x2 = x1 + _dot(o.astype(_BF), wco_ref[...])
    y_ref[...] = _mlp_tail(x2, gm_ref, w1_ref, w2_ref, gf_ref, fc)


def _post(x, attn, ssm, ssm_map, mkb, mvb, woa, wos, gc, wcq, wco, gm, w1, w2, gf, *, nb, nt, tm):
    n, d = x.shape
    row = lambda bi, ti: (bi * nt + ti, 0)
    const = lambda bi, ti: (0, 0)
    mem = lambda bi, ti: (bi, 0, 0)
    resident = lambda z: pl.BlockSpec(z.shape, const, pipeline_mode=pl.Buffered(1))
    return pl.pallas_call(
        functools.partial(_post_kernel, fc=min(w1.shape[1], 1024)),
        grid=(nb, nt),
        in_specs=[pl.BlockSpec((tm, d), row),
                  pl.BlockSpec((tm, attn.shape[1]), row),
                  pl.BlockSpec((tm, wos.shape[0]), ssm_map),
                  pl.BlockSpec((None,) + mkb.shape[1:], mem), pl.BlockSpec((None,) + mvb.shape[1:], mem),
                  resident(woa), resident(wos), resident(gc), resident(wcq), resident(wco),
                  resident(gm), resident(w1), resident(w2), resident(gf)],
        out_specs=pl.BlockSpec((tm, d), row),
        out_shape=jax.ShapeDtypeStruct((n, d), _F32),
        compiler_params=_params(("parallel", "parallel")),
        name="post",
    )(x, attn, ssm, mkb, mvb, woa, wos, gc, wcq, wco, gm, w1, w2, gf)


def _mlp(x1, o, wco, gm, w1, w2, gf, *, tm):
    n, d = x1.shape
    row = lambda i: (i, 0)
    const = lambda i: (0, 0)
    return pl.pallas_call(
        functools.partial(_mlp_kernel, fc=min(w1.shape[1], 1024)),
        grid=(n // tm,),
        in_specs=[pl.BlockSpec((tm, d), row), pl.BlockSpec((tm, o.shape[1]), row),
                  pl.BlockSpec(wco.shape, const), pl.BlockSpec((1, d), const),
                  pl.BlockSpec(w1.shape, const), pl.BlockSpec(w2.shape, const),
                  pl.BlockSpec((1, d), const)],
        out_specs=pl.BlockSpec((tm, d), row),
        out_shape=jax.ShapeDtypeStruct((n, d), _F32),
        compiler_params=_params(("parallel",)),
        name="mlp",
    )(x1, o, wco, gm, w1, w2, gf)


def kernel(x_prompt, x_sample, cache_k, cache_v, state_ssm_re, state_ssm_im, cache_mem_k, cache_mem_v,
           page_table, mem_prompt, g_mix, w_in, ssm_a_re, ssm_a_im, ssm_log_dt, ssm_b_re, ssm_b_im,
           ssm_c_re, ssm_c_im, ssm_d, w_glu, b_glu, w_out, g_mem, w_ck, w_cv, g_cross, w_cq, w_co,
           g_mlp, w_ff1, w_ff2, g_final):
    depth = w_in.shape[0]
    assert depth == 1, "single-layer trunk"
    b, t, d = x_prompt.shape
    nseq, dec, _ = x_sample.shape
    a = ATTN_WIDTH
    spg = SEQS_PER_GROUP
    blk = MOBA_BLOCK
    assert t % blk == 0 and b % spg == 0 and nseq % spg == 0 and dec == spg
    n_phys, page = cache_k.shape[1], cache_k.shape[2]
    n_pages = page_table.shape[1]
    past = n_pages * page
    assert past % blk == 0 and blk % page == 0 and dec <= page
    nb = t // blk
    nbp = -(-nb // 8) * 8
    groups, ns_p = ssm_a_re.shape[1], ssm_a_re.shape[2]
    ns = groups * ns_p
    sw = groups * SSM_GROUP
    mtok = mem_prompt.shape[1]
    row2 = lambda z: z.reshape(1, -1)
    hd = (ATTN_HEADS, HEAD_DIM)

    w_in_b = w_in[0].astype(_BF)
    wo_a = w_out[0, :a].astype(_BF)
    wo_s = w_out[0, a:].astype(_BF)
    w_cq_b, w_co_b = w_cq[0].astype(_BF), w_co[0].astype(_BF)
    w1_b, w2_b = w_ff1[0].astype(_BF), w_ff2[0].astype(_BF)
    w_glu_b = w_glu[0].astype(_BF)
    w_ckv_b = jnp.concatenate([w_ck[0], w_cv[0]], axis=1).astype(_BF)
    lbr, lbi, bbr, bbi = _s5_disc(ssm_a_re[0], ssm_a_im[0], ssm_log_dt[0], ssm_b_re[0], ssm_b_im[0])
    lbr, lbi = row2(lbr), row2(lbi)
    bb_r = _block_diag(bbr.reshape(groups, SSM_GROUP, ns_p)).astype(_BF)
    bb_i = _block_diag(bbi.reshape(groups, SSM_GROUP, ns_p)).astype(_BF)
    c_r = _block_diag(jnp.transpose(ssm_c_re[0], (0, 2, 1))).astype(_BF)
    c_i = _block_diag(jnp.transpose(ssm_c_im[0], (0, 2, 1))).astype(_BF)
    s5_w = (lbr, lbi, bb_r, bb_i, c_r, c_i, row2(ssm_d[0]), w_glu_b, row2(b_glu[0]))

    tm = 512 if t % 512 == 0 else blk
    q_p, kt_p, vt_p, kb_p, vtb_p, u_p, km_p = _in_proj_prompt(x_prompt, row2(g_mix[0]), w_in_b, tm=tm)
    km = km_p.reshape(b, nb, ATTN_HEADS, HEAD_DIM)
    km = jnp.pad(jnp.transpose(km, (0, 2, 1, 3)), ((0, 0), (0, 0), (0, nbp - nb), (0, 0)))
    kmblk = (km[:, :, :, None, :] * jnp.eye(ATTN_HEADS, dtype=_F32)[None, :, None, :, None]
             ).reshape(b, ATTN_HEADS * nbp, a)
    attn_p = _moba_prompt(q_p, kb_p.reshape(b, t, a), vtb_p, kmblk)

    zeros_state = jnp.zeros((b // spg, spg, ns), _F32)
    tc = 64 if t % 64 == 0 else t
    ssm_p, hr_p, hi_p = _s5(u_p.reshape(b // spg, t * spg, sw), zeros_state, zeros_state, *s5_w, tc=tc)

    nt = t // tm
    ssm_map_p = lambda bi, ti: (bi // spg * nt + ti, bi % spg)
    mk_p, mv_p, mkb_p, mvb_p = _mem_kv(mem_prompt.reshape(b * mtok, d), row2(g_mem[0]), w_ckv_b, tm=mtok)
    mem4 = (b, mtok, MEM_HEADS, MEM_HEAD_DIM)
    mw = MEM_HEADS * MEM_HEAD_DIM
    y_p = _post(x_prompt.reshape(b * t, d), attn_p, ssm_p.reshape(b // spg * t, spg * sw), ssm_map_p,
                mkb_p.reshape(b, mtok, mw), mvb_p.reshape(b, mtok, mw),
                wo_a, wo_s, row2(g_cross[0]), w_cq_b, w_co_b, row2(g_mlp[0]), w1_b, w2_b, row2(g_final),
                nb=b, nt=nt, tm=tm)

    n_s = nseq * dec
    tms = 256 if n_s % 256 == 0 else n_s
    q_s, k_s, v_s, u_s = _norm_proj(x_sample.reshape(n_s, d), row2(g_mix[0]), w_in_b, (a, a, a, sw),
                                    tm=tms, name="in_proj_sample")
    nq = ATTN_HEADS * dec
    head_of_row = jnp.arange(nq) // dec
    head_of_col = jnp.arange(a) // HEAD_DIM
    head_mask = (head_of_row[:, None] == head_of_col[None, :]).astype(_F32)
    wq = jnp.tile(q_s.reshape(nseq, dec, a), (1, ATTN_HEADS, 1)) * head_mask[None]
    pad_new = lambda z: jnp.pad(z.reshape(nseq, dec, a), ((0, 0), (0, page - dec), (0, 0))).astype(_BF)
    slope_rows = jnp.asarray(_slopes(), _F32)[head_of_row].reshape(nq, 1)
    qpos_rows = (past + jnp.arange(nq) % dec).astype(_F32).reshape(nq, 1)
    token_minor = lambda c: jnp.transpose(c[0], (0, 2, 3, 1)).reshape(n_phys, a, page)
    pg = 16 if n_pages % 16 == 0 else blk // page
    attn_s = _moba_sample(page_table, wq, pad_new(k_s), pad_new(v_s), slope_rows, qpos_rows,
                          token_minor(cache_k), token_minor(cache_v), pg=pg, past=past)

    to_groups = lambda z: jnp.transpose(z.reshape(nseq // spg, spg, dec, -1), (0, 2, 1, 3))
    u_g = to_groups(u_s).reshape(nseq // spg, dec * spg, sw)
    h0r = state_ssm_re[0].reshape(nseq // spg, spg, ns)
    h0i = state_ssm_im[0].reshape(nseq // spg, spg, ns)
    ssm_g, hr_s, hi_s = _s5(u_g, h0r, h0i, *s5_w, tc=dec)
    ssm_s = jnp.transpose(ssm_g.reshape(nseq // spg, dec, spg, sw), (0, 2, 1, 3)).reshape(n_s, sw)

    nts = n_s // tms
    x1_s, cq_s = _mix(x_sample.reshape(n_s, d), attn_s.reshape(n_s, a), ssm_s, lambda bi, ti: (ti, 0),
                      wo_a, wo_s, row2(g_cross[0]), w_cq_b, nb=1, nt=nts, tm=tms)
    o_s = _cross_attn(cq_s, cache_mem_k[0], cache_mem_v[0], sb=spg, rq=dec, nt=1)
    y_s = _mlp(x1_s, o_s, w_co_b, row2(g_mlp[0]), w1_b, w2_b, row2(g_final), tm=tms)

    kv_out = lambda z: jnp.transpose(z.reshape((1, b) + hd + (t,)), (0, 1, 4, 2, 3))
    return (y_p.reshape(b, t, d), y_s.reshape(nseq, dec, d),
            kv_out(kt_p), kv_out(vt_p),
            hr_p.reshape(1, b, groups, ns_p), hi_p.reshape(1, b, groups, ns_p),
            mk_p.reshape((1,) + mem4), mv_p.reshape((1,) + mem4),
            k_s.reshape((1, nseq, dec) + hd), v_s.reshape((1, nseq, dec) + hd),
            hr_s.reshape(1, nseq, groups, ns_p), hi_s.reshape(1, nseq, groups, ns_p))
```

```python
import functools

import jax
import jax.numpy as jnp
from jax import lax
from jax.experimental import pallas as pl
from jax.experimental.pallas import tpu as pltpu

ATTN_HEADS = 8
HEAD_DIM = 64
ATTN_WIDTH = ATTN_HEADS * HEAD_DIM
MOBA_BLOCK = 256
MOBA_TOPK = 3
SSM_GROUP = 16
MEM_HEADS = 4
MEM_HEAD_DIM = 128
RMS_EPS = 1e-6
SEQS_PER_GROUP = 8
LANES = 128
MXU_N = 256
VMEM_LIMIT = 56 * 1024 * 1024

_BF = jnp.bfloat16
_F32 = jnp.float32
_NEG = -1e30
_BIG = 1e30
_LOG2E = 1.4426950408889634
_NT = (((1,), (1,)), ((), ()))


def _slopes():
    return [2.0 ** (-8.0 * (h + 1) / ATTN_HEADS) for h in range(ATTN_HEADS)]


def _rms(x, g):
    return x * lax.rsqrt(jnp.mean(x * x, axis=-1, keepdims=True) + RMS_EPS) * g


def _dot(a, b):
    return jnp.dot(a, b, preferred_element_type=_F32)


def _dot_nt(a, b):
    return lax.dot_general(a, b, _NT, preferred_element_type=_F32)


def _split_bf16(x):
    hi = x.astype(_BF)
    lo = (x - hi.astype(_F32)).astype(_BF)
    return hi, lo


def _dot_nt_3pass(a, b):
    a_hi, a_lo = _split_bf16(a)
    b_hi, b_lo = _split_bf16(b)
    return _dot_nt(a_hi, b_hi) + _dot_nt(a_hi, b_lo) + _dot_nt(a_lo, b_hi)


def _rank_update(rank, gm, g, tie):
    return rank + jnp.where(gm > g, 1.0, jnp.where(gm == g, tie, 0.0))


def _params(sem):
    return pltpu.CompilerParams(dimension_semantics=sem, vmem_limit_bytes=VMEM_LIMIT)


def _in_proj_prompt_kernel(x_ref, g_ref, w_ref, q_ref, kt_ref, vt_ref, kb_ref, vtb_ref, u_ref, km_ref):
    a = ATTN_WIDTH
    hn = _rms(x_ref[...], g_ref[...]).astype(_BF)
    q_ref[...] = _dot(hn, w_ref[:, 0:a])
    k = _dot(hn, w_ref[:, a:2 * a])
    v = _dot(hn, w_ref[:, 2 * a:3 * a])
    kt_ref[...] = k.T
    vt = v.T
    vt_ref[...] = vt
    kb_ref[...] = k.astype(_BF)
    u_ref[...] = _dot(hn, w_ref[:, 3 * a:])
    for r in range(k.shape[0] // MOBA_BLOCK):
        rows = slice(r * MOBA_BLOCK, (r + 1) * MOBA_BLOCK)
        vtb_ref[r] = vt[:, rows].astype(_BF)
        km_ref[r] = jnp.mean(k[rows, :], axis=0, keepdims=True)


def _in_proj_prompt(x, g, w, *, tm):
    b, t, d = x.shape
    a = ATTN_WIDTH
    su = w.shape[1] - 3 * a
    nt = t // tm
    nbt = tm // MOBA_BLOCK
    nbg = b // SEQS_PER_GROUP
    row = lambda bi, ti: (bi * nt + ti, 0)
    tok_minor = lambda bi, ti: (bi, 0, ti)
    return pl.pallas_call(
        _in_proj_prompt_kernel,
        grid=(b, nt),
        in_specs=[pl.BlockSpec((tm, d), row),
                  pl.BlockSpec((1, d), lambda bi, ti: (0, 0)),
                  pl.BlockSpec(w.shape, lambda bi, ti: (0, 0))],
        out_specs=[pl.BlockSpec((tm, a), row),
                   pl.BlockSpec((None, a, tm), tok_minor),
                   pl.BlockSpec((None, a, tm), tok_minor),
                   pl.BlockSpec((tm, a), row),
                   pl.BlockSpec((None, nbt, a, MOBA_BLOCK), lambda bi, ti: (bi, ti, 0, 0)),
                   pl.BlockSpec((None, tm, su), lambda bi, ti: (bi // SEQS_PER_GROUP, ti, bi % SEQS_PER_GROUP)),
                   pl.BlockSpec((nbt, 1, a), lambda bi, ti: (bi * nt + ti, 0, 0))],
        out_shape=[jax.ShapeDtypeStruct((b * t, a), _F32),
                   jax.ShapeDtypeStruct((b, a, t), _F32),
                   jax.ShapeDtypeStruct((b, a, t), _F32),
                   jax.ShapeDtypeStruct((b * t, a), _BF),
                   jax.ShapeDtypeStruct((b, t // MOBA_BLOCK, a, MOBA_BLOCK), _BF),
                   jax.ShapeDtypeStruct((nbg, t, SEQS_PER_GROUP * su), _F32),
                   jax.ShapeDtypeStruct((b * t // MOBA_BLOCK, 1, a), _F32)],
        compiler_params=_params(("parallel", "parallel")),
        name="in_proj_prompt",
    )(x.reshape(b * t, d), g, w)


def _norm_proj_kernel(x_ref, g_ref, w_ref, *o_refs):
    hn = _rms(x_ref[...], g_ref[...]).astype(_BF)
    c0 = 0
    for o_ref in o_refs:
        c1 = c0 + o_ref.shape[1]
        o_ref[...] = _dot(hn, w_ref[:, c0:c1])
        c0 = c1


def _norm_proj(x, g, w, widths, *, tm, name):
    n, d = x.shape
    row = lambda i: (i, 0)
    return pl.pallas_call(
        _norm_proj_kernel,
        grid=(n // tm,),
        in_specs=[pl.BlockSpec((tm, d), row),
                  pl.BlockSpec((1, d), lambda i: (0, 0)),
                  pl.BlockSpec(w.shape, lambda i: (0, 0))],
        out_specs=[pl.BlockSpec((tm, c), row) for c in widths],
        out_shape=[jax.ShapeDtypeStruct((n, c), _F32) for c in widths],
        compiler_params=_params(("parallel",)),
        name=name,
    )(x, g, w)


def _moba_prompt_kernel(q_ref, k_ref, vt_ref, km_ref, o_ref,
                        sel_ref, bias_ref, qx_ref, sa_ref, sb_ref, m_ref, l_ref, acc_ref, *, nb, nbp):
    blk = MOBA_BLOCK
    nh = ATTN_HEADS
    i = pl.program_id(1)
    q = q_ref[...]
    slopes2 = [s * _LOG2E for s in _slopes()]
    kk = lax.broadcasted_iota(jnp.int32, (blk, blk), 0)
    qq = lax.broadcasted_iota(jnp.int32, (blk, blk), 1)
    dkq = (kk - qq).astype(_F32)

    @pl.when(i == 0)
    def _fill_bias():
        for h in range(nh):
            bias_ref[h] = slopes2[h] * dkq

    gt = _dot_nt_3pass(km_ref[...], q)
    rows = lax.broadcasted_iota(jnp.int32, (nbp, blk), 0)
    fully_past = rows < i
    g = [jnp.where(fully_past, gt[h * nbp:(h + 1) * nbp, :], _NEG) for h in range(nh)]
    rank = [jnp.zeros((nbp, blk), _F32) for _ in range(nh)]
    for m in range(nb):
        tie = jnp.where(rows > m, 1.0, 0.0)
        for h in range(nh):
            rank[h] = _rank_update(rank[h], g[h][m:m + 1, :], g[h], tie)
    for h in range(nh):
        sel_ref[h * nbp:(h + 1) * nbp, :] = jnp.where(fully_past & (rank[h] < MOBA_TOPK), 1.0, 0.0)

    qs = (q * (HEAD_DIM ** -0.5 * _LOG2E)).astype(_BF)
    lane = lax.broadcasted_iota(jnp.int32, (blk, LANES), 1)
    zero = jnp.zeros((), _BF)
    for p in range(nh // 2):
        qp = qs[:, p * LANES:(p + 1) * LANES]
        qx_ref[p, 0:blk, :] = jnp.where(lane < HEAD_DIM, qp, zero)
        qx_ref[p, blk:2 * blk, :] = jnp.where(lane >= HEAD_DIM, qp, zero)

    causal_neg = jnp.where(dkq <= 0.0, 0.0, _NEG)
    ones_rows = jnp.ones((8, blk), _BF)

    def scores(j, s_out, p):
        start = pl.multiple_of(j * blk, blk)
        kb = k_ref[pl.ds(start, blk), p * LANES:(p + 1) * LANES]
        s_out[p] = _dot_nt(kb, qx_ref[p])

    def step(j_next, s_next, j, s_cur, own):
        coff = None if own else ((i - j) * blk).astype(_F32)
        for p in range(nh // 2):
            if j_next is not None:
                scores(j_next, s_next, p)
            vb = vt_ref[j, p * LANES:(p + 1) * LANES, :]
            ptbs = []
            alphas = []
            for x in range(2):
                h = 2 * p + x
                s = s_cur[p, :, x * blk:(x + 1) * blk] + bias_ref[h]
                if own:
                    s = s + causal_neg
                    m_new = jnp.max(s, axis=0, keepdims=True)
                    t = m_new
                    alphas.append(None)
                else:
                    c = slopes2[h] * coff
                    picked = sel_ref[pl.ds(h * nbp + j, 1), :] > 0.5
                    m_old = m_ref[h]
                    m_new = jnp.where(picked, jnp.maximum(m_old, jnp.max(s, axis=0, keepdims=True) - c), m_old)
                    t = jnp.where(picked, m_new + c, _BIG)
                    alphas.append(jnp.exp2(m_old - m_new))
                m_ref[h] = m_new
                ptbs.append(jnp.exp2(s - t).astype(_BF))
            for x in range(2):
                h = 2 * p + x
                arow = slice(h * HEAD_DIM, (h + 1) * HEAD_DIM)
                psum = _dot(ones_rows, ptbs[x])[0:1, :]
                pv = _dot(vb[x * HEAD_DIM:(x + 1) * HEAD_DIM, :], ptbs[x])
                if own:
                    l_ref[h] = psum
                    acc_ref[arow, :] = pv
                else:
                    l_ref[h] = alphas[x] * l_ref[h] + psum
                    acc_ref[arow, :] = acc_ref[arow, :] * alphas[x] + pv

    for p in range(nh // 2):
        scores(i, sa_ref, p)
    step(0, sb_ref, i, sa_ref, True)
    last = jnp.maximum(i - 1, 0)

    def pair(mi, carry):
        b0 = 2 * mi
        step(b0 + 1, sa_ref, b0, sb_ref, False)
        step(jnp.minimum(b0 + 2, last), sb_ref, b0 + 1, sa_ref, False)
        return carry

    lax.fori_loop(0, i // 2, pair, 0)

    @pl.when(i % 2 == 1)
    def _odd_tail():
        step(None, None, i - 1, sb_ref, False)

    for h in range(nh):
        arow = slice(h * HEAD_DIM, (h + 1) * HEAD_DIM)
        acc_ref[arow, :] = acc_ref[arow, :] * (1.0 / l_ref[h])
    o_ref[...] = acc_ref[...].T


def _moba_prompt(q, kb, vt, kmblk):
    b, t, a = kb.shape
    blk = MOBA_BLOCK
    nb = t // blk
    nbp = kmblk.shape[1] // ATTN_HEADS
    return pl.pallas_call(
        functools.partial(_moba_prompt_kernel, nb=nb, nbp=nbp),
        grid=(b, nb),
        in_specs=[pl.BlockSpec((blk, a), lambda bi, i: (bi * nb + i, 0)),
                  pl.BlockSpec((None, t, a), lambda bi, i: (bi, 0, 0)),
                  pl.BlockSpec((None, nb, a, blk), lambda bi, i: (bi, 0, 0, 0)),
                  pl.BlockSpec((None,) + kmblk.shape[1:], lambda bi, i: (bi, 0, 0))],
        out_specs=pl.BlockSpec((blk, a), lambda bi, i: (bi * nb + i, 0)),
        out_shape=jax.ShapeDtypeStruct((b * t, a), _F32),
        scratch_shapes=[pltpu.VMEM((ATTN_HEADS * nbp, blk), _F32),
                        pltpu.VMEM((ATTN_HEADS, blk, blk), _F32),
                        pltpu.VMEM((ATTN_HEADS // 2, 2 * blk, LANES), _BF),
                        pltpu.VMEM((ATTN_HEADS // 2, blk, 2 * blk), _F32),
                        pltpu.VMEM((ATTN_HEADS // 2, blk, 2 * blk), _F32),
                        pltpu.VMEM((ATTN_HEADS, 1, blk), _F32),
                        pltpu.VMEM((ATTN_HEADS, 1, blk), _F32),
                        pltpu.VMEM((a, blk), _F32)],
        compiler_params=_params(("parallel", "arbitrary")),
        name="moba_prompt",
    )(q, kb, vt, kmblk)


def _moba_sample_kernel(pt_ref, wq_ref, kn_ref, vn_ref, slope_ref, qpos_ref, *rest,
                        pg, n_pages, page, past, dec):
    del pt_ref
    k_refs = rest[:pg]
    v_refs = rest[pg:2 * pg]
    o_ref, s_ref, gate_ref, acc_ref, l_ref = rest[2 * pg:]
    blk = MOBA_BLOCK
    ppb = blk // page
    nb = n_pages // ppb
    n_groups = n_pages // pg
    step = pl.program_id(1)
    wq = wq_ref[...]
    nq = wq.shape[0]
    wqb = (wq * (HEAD_DIM ** -0.5 * _LOG2E)).astype(_BF)
    blk_lane = lax.broadcasted_iota(jnp.int32, (nq, LANES), 1)

    @pl.when(step == 0)
    def _init():
        gate_ref[...] = jnp.zeros(gate_ref.shape, _F32)

    @pl.when(step < n_groups)
    def _k_phase():
        for c in range(pg // ppb):
            gs = jnp.zeros((nq, page), _F32)
            for r in range(c * ppb, (c + 1) * ppb):
                st = _dot(wqb, k_refs[r][...].astype(_BF))
                s_ref[step * pg + r] = st
                gs = gs + st
            n = step * (pg // ppb) + c
            gsum = jnp.sum(gs, axis=1, keepdims=True)
            gate_ref[...] = jnp.where(blk_lane == n, gsum, gate_ref[...])

    @pl.when(step == n_groups - 1)
    def _softmax():
        gate = gate_ref[...]
        rank = jnp.zeros(gate.shape, _F32)
        for m in range(nb):
            tie = jnp.where(blk_lane > m, 1.0, 0.0)
            rank = _rank_update(rank, gate[:, m:m + 1], gate, tie)
        drop = jnp.where(rank < MOBA_TOPK, 0.0, _NEG)
        lane = lax.broadcasted_iota(jnp.int32, (nq, page), 1).astype(_F32)
        slope2 = jnp.broadcast_to(slope_ref[...] * _LOG2E, (nq, page))
        qpos = jnp.broadcast_to(qpos_ref[...], (nq, page))
        base = slope2 * (lane - qpos)

        mm = jnp.full((nq, page), _NEG, _F32)
        for p in range(n_pages):
            n = p // ppb
            z = s_ref[p] + (base + slope2 * float(p * page)) + drop[:, n:n + 1]
            s_ref[p] = z
            mm = jnp.maximum(mm, z)
        so = _dot_nt(wqb, kn_ref[...]) + (base + slope2 * float(past))
        so = jnp.where(lane + float(past) <= qpos, so, _NEG)
        m = jnp.max(jnp.maximum(mm, so), axis=1, keepdims=True)
        ls = jnp.exp2(so - m)
        acc_ref[...] = _dot(ls.astype(_BF), vn_ref[...])
        for p in range(n_pages):
            pp = jnp.exp2(s_ref[p] - m)
            s_ref[p] = pp
            ls = ls + pp
        l_ref[...] = jnp.sum(ls, axis=1, keepdims=True)

    @pl.when(step >= n_groups)
    def _v_phase():
        acc = acc_ref[...]
        for r in range(pg):
            pp = s_ref[(step - n_groups) * pg + r]
            acc = acc + _dot_nt(pp.astype(_BF), v_refs[r][...].astype(_BF))
        acc_ref[...] = acc

    @pl.when(step == 2 * n_groups - 1)
    def _finish():
        o = acc_ref[...] * (1.0 / l_ref[...])
        lane_a = lax.broadcasted_iota(jnp.int32, (dec, o.shape[1]), 1)
        out = jnp.zeros((dec, o.shape[1]), _F32)
        for h in range(ATTN_HEADS):
            in_head = (lane_a >= h * HEAD_DIM) & (lane_a < (h + 1) * HEAD_DIM)
            out = out + jnp.where(in_head, o[h * dec:(h + 1) * dec, :], 0.0)
        o_ref[...] = out


def _moba_sample(page_table, wq, kn, vn, slope_rows, qpos_rows, cache_kt, cache_vt, *, pg, past):
    nseq, n_pages = page_table.shape
    _, a, page = cache_kt.shape
    nq = wq.shape[1]
    dec = nq // ATTN_HEADS
    n_groups = n_pages // pg
    assert n_pages * page // MOBA_BLOCK <= LANES
    seq = lambda bi, s, pt: (bi, 0, 0)
    const = lambda bi, s, pt: (0, 0)

    def k_map(r):
        return lambda bi, s, pt: (pt[bi, jnp.minimum(s, n_groups - 1) * pg + r], 0, 0)

    def v_map(r):
        def index(bi, s, pt):
            in_v = s >= n_groups
            seq_i = jnp.where(in_v, bi, jnp.maximum(bi - 1, 0))
            grp = jnp.where(in_v, s - n_groups, n_groups - 1)
            return (pt[seq_i, grp * pg + r], 0, 0)
        return index

    in_specs = [pl.BlockSpec((None, nq, a), seq),
                pl.BlockSpec((None, page, a), seq),
                pl.BlockSpec((None, page, a), seq),
                pl.BlockSpec((nq, 1), const),
                pl.BlockSpec((nq, 1), const)]
    in_specs += [pl.BlockSpec((None, a, page), k_map(r)) for r in range(pg)]
    in_specs += [pl.BlockSpec((None, a, page), v_map(r)) for r in range(pg)]
    return pl.pallas_call(
        functools.partial(_moba_sample_kernel, pg=pg, n_pages=n_pages, page=page, past=past, dec=dec),
        grid_spec=pltpu.PrefetchScalarGridSpec(
            num_scalar_prefetch=1,
            grid=(nseq, 2 * n_groups),
            in_specs=in_specs,
            out_specs=pl.BlockSpec((None, dec, a), seq),
            scratch_shapes=[pltpu.VMEM((n_pages, nq, page), _F32),
                            pltpu.VMEM((nq, LANES), _F32),
                            pltpu.VMEM((nq, a), _F32),
                            pltpu.VMEM((nq, 1), _F32)]),
        out_shape=jax.ShapeDtypeStruct((nseq, dec, a), _F32),
        compiler_params=_params(("parallel", "arbitrary")),
        name="moba_sample",
    )(page_table, wq, kn, vn, slope_rows, qpos_rows, *([cache_kt] * pg), *([cache_vt] * pg))


def _s5_disc_kernel(are_ref, aim_ref, ldt_ref, bre_ref, bim_ref, lbr_ref, lbi_ref, bbr_ref, bbi_ref):
    are = are_ref[...]
    aim = aim_ref[...]
    dt = jnp.exp(ldt_ref[...])
    mag = jnp.exp(are * dt)
    ang = aim * dt
    lb_re = mag * jnp.cos(ang)
    lb_im = mag * jnp.sin(ang)
    e_re = lb_re - 1.0
    e_im = lb_im
    den = are * are + aim * aim
    f_re = (e_re * are + e_im * aim) / den
    f_im = (e_im * are - e_re * aim) / den
    br = bre_ref[...]
    bi = bim_ref[...]
    lbr_ref[...] = lb_re
    lbi_ref[...] = lb_im
    bbr_ref[...] = f_re * br - f_im * bi
    bbi_ref[...] = f_re * bi + f_im * br


def _s5_disc(a_re, a_im, log_dt, b_re, b_im):
    g, p, c = b_re.shape
    rep = lambda z: jnp.repeat(z, c, axis=0)
    are, aim = rep(a_re), rep(a_im)
    ldt = rep(log_dt.reshape(g, 1))
    b2 = lambda z: jnp.transpose(z, (0, 2, 1)).reshape(g * c, p)
    shp = jax.ShapeDtypeStruct((g * c, p), _F32)
    lbr, lbi, bbr, bbi = pl.pallas_call(
        _s5_disc_kernel, out_shape=[shp] * 4, name="s5_disc",
    )(are, aim, ldt, b2(b_re), b2(b_im))
    return lbr[::c], lbi[::c], bbr, bbi


def _s5_kernel(u_ref, h0r_ref, h0i_ref, lbr_ref, lbi_ref, bbr_ref, bbi_ref, cr_ref, ci_ref, d_ref,
               wg_ref, bg_ref, y_ref, hr_out, hi_out, hre, him, st_r, st_i, *, tc, cw):
    spg = SEQS_PER_GROUP
    t = pl.program_id(1)
    u = u_ref[...]
    ub = u.astype(_BF)
    w = u.shape[1]
    ns = hre.shape[1]
    ratio = ns // w
    for n in range(ns // MXU_N):
        out = slice(n * MXU_N, (n + 1) * MXU_N)
        lo = (n * MXU_N // ratio) // LANES * LANES
        src = slice(lo, lo + LANES)
        hre[:, out] = _dot(ub[:, src], bbr_ref[src, out])
        him[:, out] = _dot(ub[:, src], bbi_ref[src, out])

    @pl.when(t == 0)
    def _init():
        st_r[...] = h0r_ref[...]
        st_i[...] = h0i_ref[...]

    for c in range(ns // cw):
        cs = slice(c * cw, (c + 1) * cw)
        lr = jnp.broadcast_to(lbr_ref[:, cs], (spg, cw))
        li = jnp.broadcast_to(lbi_ref[:, cs], (spg, cw))

        def step(tt, carry):
            sr, si = carry
            r0 = pl.multiple_of(tt * spg, spg)
            nr = lr * sr - li * si + hre[pl.ds(r0, spg), cs]
            ni = lr * si + li * sr + him[pl.ds(r0, spg), cs]
            hre[pl.ds(r0, spg), cs] = nr
            him[pl.ds(r0, spg), cs] = ni
            return nr, ni

        sr, si = lax.fori_loop(0, tc, step, (st_r[:, cs], st_i[:, cs]))
        st_r[:, cs] = sr
        st_i[:, cs] = si

    parts = []
    for m in range(w // MXU_N):
        out = slice(m * MXU_N, (m + 1) * MXU_N)
        src = slice(m * MXU_N * ratio, (m + 1) * MXU_N * ratio)
        parts.append(_dot(hre[:, src].astype(_BF), cr_ref[src, out])
                     - _dot(him[:, src].astype(_BF), ci_ref[src, out]))
    y = jnp.concatenate(parts, axis=1) + d_ref[...] * u
    z = _dot(y.astype(_BF), wg_ref[...]) + bg_ref[...]
    y_ref[...] = y * (1.0 / (1.0 + jnp.exp(-z)))
    hr_out[...] = st_r[...]
    hi_out[...] = st_i[...]


def _s5(u, h0r, h0i, lbr, lbi, bbr, bbi, cr, ci, d, wg, bg, *, tc):
    nbg, rows, w = u.shape
    spg = SEQS_PER_GROUP
    t = rows // spg
    ns = lbr.shape[1]
    ratio = ns // w
    assert ns % MXU_N == 0 and w % MXU_N == 0 and LANES % (MXU_N // ratio) == 0
    const = lambda gi, ti: (0, 0)
    grp = lambda gi, ti: (gi, 0, 0)
    return pl.pallas_call(
        functools.partial(_s5_kernel, tc=tc, cw=min(ns, 4 * LANES)),
        grid=(nbg, t // tc),
        in_specs=[pl.BlockSpec((None, tc * spg, w), lambda gi, ti: (gi, ti, 0)),
                  pl.BlockSpec((None, spg, ns), grp),
                  pl.BlockSpec((None, spg, ns), grp),
                  pl.BlockSpec((1, ns), const), pl.BlockSpec((1, ns), const),
                  pl.BlockSpec((w, ns), const), pl.BlockSpec((w, ns), const),
                  pl.BlockSpec((ns, w), const), pl.BlockSpec((ns, w), const),
                  pl.BlockSpec((1, w), const),
                  pl.BlockSpec((w, w), const), pl.BlockSpec((1, w), const)],
        out_specs=[pl.BlockSpec((None, tc * spg, w), lambda gi, ti: (gi, ti, 0)),
                   pl.BlockSpec((None, spg, ns), grp),
                   pl.BlockSpec((None, spg, ns), grp)],
        out_shape=[jax.ShapeDtypeStruct((nbg, rows, w), _F32),
                   jax.ShapeDtypeStruct((nbg, spg, ns), _F32),
                   jax.ShapeDtypeStruct((nbg, spg, ns), _F32)],
        scratch_shapes=[pltpu.VMEM((tc * spg, ns), _F32), pltpu.VMEM((tc * spg, ns), _F32),
                        pltpu.VMEM((spg, ns), _F32), pltpu.VMEM((spg, ns), _F32)],
        compiler_params=_params(("parallel", "arbitrary")),
        name="s5",
    )(u, h0r, h0i, lbr, lbi, bbr, bbi, cr, ci, d, wg, bg)


def _block_diag(blocks):
    g, r, c = blocks.shape
    eye = jnp.eye(g, dtype=blocks.dtype)
    return (blocks[:, :, None, :] * eye[:, None, :, None]).reshape(g * r, g * c)


def _mix_kernel(x_ref, a_ref, s_ref, woa_ref, wos_ref, gc_ref, wcq_ref, x1_ref, cq_ref):
    x1 = (x_ref[...] + _dot(a_ref[...].astype(_BF), woa_ref[...])
          + _dot(s_ref[...].astype(_BF), wos_ref[...]))
    x1_ref[...] = x1
    cq_ref[...] = _dot(_rms(x1, gc_ref[...]).astype(_BF), wcq_ref[...])


def _mix(x, attn, ssm, ssm_map, woa, wos, gc, wcq, *, nb, nt, tm):
    n, d = x.shape
    row = lambda bi, ti: (bi * nt + ti, 0)
    const = lambda bi, ti: (0, 0)
    mw = wcq.shape[1]
    sw = wos.shape[0]
    return pl.pallas_call(
        _mix_kernel,
        grid=(nb, nt),
        in_specs=[pl.BlockSpec((tm, d), row),
                  pl.BlockSpec((tm, attn.shape[1]), row),
                  pl.BlockSpec((tm, sw), ssm_map),
                  pl.BlockSpec(woa.shape, const), pl.BlockSpec(wos.shape, const),
                  pl.BlockSpec((1, d), const), pl.BlockSpec(wcq.shape, const)],
        out_specs=[pl.BlockSpec((tm, d), row), pl.BlockSpec((tm, mw), row)],
        out_shape=[jax.ShapeDtypeStruct((n, d), _F32), jax.ShapeDtypeStruct((n, mw), _F32)],
        compiler_params=_params(("parallel", "parallel")),
        name="mix",
    )(x, attn, ssm, woa, wos, gc, wcq)


def _mem_kv_kernel(x_ref, g_ref, w_ref, mk_ref, mv_ref, mkb_ref, mvb_ref):
    hn = _rms(x_ref[...], g_ref[...]).astype(_BF)
    mw = MEM_HEADS * MEM_HEAD_DIM
    k = _dot(hn, w_ref[:, :mw])
    v = _dot(hn, w_ref[:, mw:])
    mkb_ref[...] = k.astype(_BF)
    mvb_ref[...] = v.astype(_BF)
    for h in range(MEM_HEADS):
        cols = slice(h * MEM_HEAD_DIM, (h + 1) * MEM_HEAD_DIM)
        mk_ref[:, h, :] = k[:, cols]
        mv_ref[:, h, :] = v[:, cols]


def _mem_kv(mem, g, w, *, tm):
    n, d = mem.shape
    mw = MEM_HEADS * MEM_HEAD_DIM
    shp = jax.ShapeDtypeStruct((n, MEM_HEADS, MEM_HEAD_DIM), _F32)
    spec = pl.BlockSpec((tm, MEM_HEADS, MEM_HEAD_DIM), lambda i: (i, 0, 0))
    flat = pl.BlockSpec((tm, mw), lambda i: (i, 0))
    return pl.pallas_call(
        _mem_kv_kernel,
        grid=(n // tm,),
        in_specs=[pl.BlockSpec((tm, d), lambda i: (i, 0)),
                  pl.BlockSpec((1, d), lambda i: (0, 0)),
                  pl.BlockSpec(w.shape, lambda i: (0, 0))],
        out_specs=[spec, spec, flat, flat],
        out_shape=[shp, shp, jax.ShapeDtypeStruct((n, mw), _BF), jax.ShapeDtypeStruct((n, mw), _BF)],
        compiler_params=_params(("parallel",)),
        name="mem_kv",
    )(mem, g, w)


def _cross_attn_kernel(q_ref, mk_ref, mv_ref, o_ref, *, sb, rq):
    nrow = MEM_HEADS * rq
    nmem = mk_ref.shape[1]
    row_head = lax.broadcasted_iota(jnp.int32, (nrow, nmem), 0) // rq
    col_head = lax.broadcasted_iota(jnp.int32, (nrow, nmem), 1) % MEM_HEADS
    same_head = row_head == col_head
    for s in range(sb):
        q = q_ref[s * rq:(s + 1) * rq, :]
        wq = jnp.concatenate([q[:, h * MEM_HEAD_DIM:(h + 1) * MEM_HEAD_DIM] for h in range(MEM_HEADS)], axis=0)
        sc = _dot_nt(wq.astype(_BF), mk_ref[s].astype(_BF)) * (MEM_HEAD_DIM ** -0.5)
        sc = jnp.where(same_head, sc, _NEG)
        m = jnp.max(sc, axis=-1, keepdims=True)
        e = jnp.exp(sc - m)
        p = e * (1.0 / jnp.sum(e, axis=-1, keepdims=True))
        o = _dot(p.astype(_BF), mv_ref[s].astype(_BF))
        o_ref[s * rq:(s + 1) * rq, :] = jnp.concatenate(
            [o[h * rq:(h + 1) * rq, :] for h in range(MEM_HEADS)], axis=1)


def _cross_attn(cq, mk, mv, *, sb, rq):
    n, mw = cq.shape
    nseq, nmem, dh = mk.shape
    mem = lambda si: (si, 0, 0)
    return pl.pallas_call(
        functools.partial(_cross_attn_kernel, sb=sb, rq=rq),
        grid=(nseq // sb,),
        in_specs=[pl.BlockSpec((sb * rq, mw), lambda si: (si, 0)),
                  pl.BlockSpec((sb, nmem, dh), mem),
                  pl.BlockSpec((sb, nmem, dh), mem)],
        out_specs=pl.BlockSpec((sb * rq, mw), lambda si: (si, 0)),
        out_shape=jax.ShapeDtypeStruct((n, mw), _F32),
        compiler_params=_params(("parallel",)),
        name="cross_attn",
    )(cq, mk, mv)


def _mlp_tail(x2, gm_ref, w1_ref, w2_ref, gf_ref, fc):
    hn = _rms(x2, gm_ref[...]).astype(_BF)
    acc = x2
    for c in range(w1_ref.shape[1] // fc):
        cols = slice(c * fc, (c + 1) * fc)
        hid = jnp.square(jnp.maximum(_dot(hn, w1_ref[:, cols]), 0.0)).astype(_BF)
        acc = acc + _dot(hid, w2_ref[cols, :])
    return _rms(acc, gf_ref[...])


def _mlp_kernel(x1_ref, o_ref, wco_ref, gm_ref, w1_ref, w2_ref, gf_ref, y_ref, *, fc):
    x2 = x1_ref[...] + _dot(o_ref[...].astype(_BF), wco_ref[...])
    y_ref[...] = _mlp_tail(x2, gm_ref, w1_ref, w2_ref, gf_ref, fc)


def _post_kernel(x_ref, a_ref, s_ref, mk_ref, mv_ref, woa_ref, wos_ref, gc_ref, wcq_ref, wco_ref,
                 gm_ref, w1_ref, w2_ref, gf_ref, y_ref, *, fc):
    x1 = (x_ref[...] + _dot(a_ref[...].astype(_BF), woa_ref[...])
          + _dot(s_ref[...].astype(_BF), wos_ref[...]))
    cq = _dot(_rms(x1, gc_ref[...]).astype(_BF), wcq_ref[...]).astype(_BF)
    heads = []
    for h in range(MEM_HEADS):
        cols = slice(h * MEM_HEAD_DIM, (h + 1) * MEM_HEAD_DIM)
        sc = _dot_nt(cq[:, cols], mk_ref[:, cols]) * (MEM_HEAD_DIM ** -0.5)
        m = jnp.max(sc, axis=-1, keepdims=True)
        e = jnp.exp(sc - m)
        p = e * (1.0 / jnp.sum(e, axis=-1, keepdims=True))
        heads.append(_dot(p.astype(_BF), mv_ref[:, cols]))
    o = jnp.concatenate(heads, axis=1)
    x2 = x1 + _dot(o.astype(_BF), wco_ref[...])
    y_ref[...] = _mlp_tail(x2, gm_ref, w1_ref, w2_ref, gf_ref, fc)


def _post(x, attn, ssm, ssm_map, mkb, mvb, woa, wos, gc, wcq, wco, gm, w1, w2, gf, *, nb, nt, tm):
    n, d = x.shape
    row = lambda bi, ti: (bi * nt + ti, 0)
    const = lambda bi, ti: (0, 0)
    mem = lambda bi, ti: (bi, 0, 0)
    resident = lambda z: pl.BlockSpec(z.shape, const, pipeline_mode=pl.Buffered(1))
    return pl.pallas_call(
        functools.partial(_post_kernel, fc=min(w1.shape[1], 1024)),
        grid=(nb, nt),
        in_specs=[pl.BlockSpec((tm, d), row),
                  pl.BlockSpec((tm, attn.shape[1]), row),
                  pl.BlockSpec((tm, wos.shape[0]), ssm_map),
                  pl.BlockSpec((None,) + mkb.shape[1:], mem), pl.BlockSpec((None,) + mvb.shape[1:], mem),
                  resident(woa), resident(wos), resident(gc), resident(wcq), resident(wco),
                  resident(gm), resident(w1), resident(w2), resident(gf)],
        out_specs=pl.BlockSpec((tm, d), row),
        out_shape=jax.ShapeDtypeStruct((n, d), _F32),
        compiler_params=_params(("parallel", "parallel")),
        name="post",
    )(x, attn, ssm, mkb, mvb, woa, wos, gc, wcq, wco, gm, w1, w2, gf)


def _mlp(x1, o, wco, gm, w1, w2, gf, *, tm):
    n, d = x1.shape
    row = lambda i: (i, 0)
    const = lambda i: (0, 0)
    return pl.pallas_call(
        functools.partial(_mlp_kernel, fc=min(w1.shape[1], 1024)),
        grid=(n // tm,),
        in_specs=[pl.BlockSpec((tm, d), row), pl.BlockSpec((tm, o.shape[1]), row),
                  pl.BlockSpec(wco.shape, const), pl.BlockSpec((1, d), const),
                  pl.BlockSpec(w1.shape, const), pl.BlockSpec(w2.shape, const),
                  pl.BlockSpec((1, d), const)],
        out_specs=pl.BlockSpec((tm, d), row),
        out_shape=jax.ShapeDtypeStruct((n, d), _F32),
        compiler_params=_params(("parallel",)),
        name="mlp",
    )(x1, o, wco, gm, w1, w2, gf)


def kernel(x_prompt, x_sample, cache_k, cache_v, state_ssm_re, state_ssm_im, cache_mem_k, cache_mem_v,
           page_table, mem_prompt, g_mix, w_in, ssm_a_re, ssm_a_im, ssm_log_dt, ssm_b_re, ssm_b_im,
           ssm_c_re, ssm_c_im, ssm_d, w_glu, b_glu, w_out, g_mem, w_ck, w_cv, g_cross, w_cq, w_co,
           g_mlp, w_ff1, w_ff2, g_final):
    depth = w_in.shape[0]
    assert depth == 1, "single-layer trunk"
    b, t, d = x_prompt.shape
    nseq, dec, _ = x_sample.shape
    a = ATTN_WIDTH
    spg = SEQS_PER_GROUP
    blk = MOBA_BLOCK
    assert t % blk == 0 and b % spg == 0 and nseq % spg == 0 and dec == spg
    n_phys, page = cache_k.shape[1], cache_k.shape[2]
    n_pages = page_table.shape[1]
    past = n_pages * page
    assert past % blk == 0 and blk % page == 0 and dec <= page
    nb = t // blk
    nbp = -(-nb // 8) * 8
    groups, ns_p = ssm_a_re.shape[1], ssm_a_re.shape[2]
    ns = groups * ns_p
    sw = groups * SSM_GROUP
    mtok = mem_prompt.shape[1]
    row2 = lambda z: z.reshape(1, -1)
    hd = (ATTN_HEADS, HEAD_DIM)

    w_in_b = w_in[0].astype(_BF)
    wo_a = w_out[0, :a].astype(_BF)
    wo_s = w_out[0, a:].astype(_BF)
    w_cq_b, w_co_b = w_cq[0].astype(_BF), w_co[0].astype(_BF)
    w1_b, w2_b = w_ff1[0].astype(_BF), w_ff2[0].astype(_BF)
    w_glu_b = w_glu[0].astype(_BF)
    w_ckv_b = jnp.concatenate([w_ck[0], w_cv[0]], axis=1).astype(_BF)
    lbr, lbi, bbr, bbi = _s5_disc(ssm_a_re[0], ssm_a_im[0], ssm_log_dt[0], ssm_b_re[0], ssm_b_im[0])
    lbr, lbi = row2(lbr), row2(lbi)
    bb_r = _block_diag(bbr.reshape(groups, SSM_GROUP, ns_p)).astype(_BF)
    bb_i = _block_diag(bbi.reshape(groups, SSM_GROUP, ns_p)).astype(_BF)
    c_r = _block_diag(jnp.transpose(ssm_c_re[0], (0, 2, 1))).astype(_BF)
    c_i = _block_diag(jnp.transpose(ssm_c_im[0], (0, 2, 1))).astype(_BF)
    s5_w = (lbr, lbi, bb_r, bb_i, c_r, c_i, row2(ssm_d[0]), w_glu_b, row2(b_glu[0]))

    tm = 512 if t % 512 == 0 else blk
    q_p, kt_p, vt_p, kb_p, vtb_p, u_p, km_p = _in_proj_prompt(x_prompt, row2(g_mix[0]), w_in_b, tm=tm)
    km = km_p.reshape(b, nb, ATTN_HEADS, HEAD_DIM)
    km = jnp.pad(jnp.transpose(km, (0, 2, 1, 3)), ((0, 0), (0, 0), (0, nbp - nb), (0, 0)))
    kmblk = (km[:, :, :, None, :] * jnp.eye(ATTN_HEADS, dtype=_F32)[None, :, None, :, None]
             ).reshape(b, ATTN_HEADS * nbp, a)
    attn_p = _moba_prompt(q_p, kb_p.reshape(b, t, a), vtb_p, kmblk)

    zeros_state = jnp.zeros((b // spg, spg, ns), _F32)
    tc = 64 if t % 64 == 0 else t
    ssm_p, hr_p, hi_p = _s5(u_p.reshape(b // spg, t * spg, sw), zeros_state, zeros_state, *s5_w, tc=tc)

    nt = t // tm
    ssm_map_p = lambda bi, ti: (bi // spg * nt + ti, bi % spg)
    mk_p, mv_p, mkb_p, mvb_p = _mem_kv(mem_prompt.reshape(b * mtok, d), row2(g_mem[0]), w_ckv_b, tm=mtok)
    mem4 = (b, mtok, MEM_HEADS, MEM_HEAD_DIM)
    mw = MEM_HEADS * MEM_HEAD_DIM
    y_p = _post(x_prompt.reshape(b * t, d), attn_p, ssm_p.reshape(b // spg * t, spg * sw), ssm_map_p,
                mkb_p.reshape(b, mtok, mw), mvb_p.reshape(b, mtok, mw),
                wo_a, wo_s, row2(g_cross[0]), w_cq_b, w_co_b, row2(g_mlp[0]), w1_b, w2_b, row2(g_final),
                nb=b, nt=nt, tm=tm)

    n_s = nseq * dec
    tms = 256 if n_s % 256 == 0 else n_s
    q_s, k_s, v_s, u_s = _norm_proj(x_sample.reshape(n_s, d), row2(g_mix[0]), w_in_b, (a, a, a, sw),
                                    tm=tms, name="in_proj_sample")
    nq = ATTN_HEADS * dec
    head_of_row = jnp.arange(nq) // dec
    head_of_col = jnp.arange(a) // HEAD_DIM
    head_mask = (head_of_row[:, None] == head_of_col[None, :]).astype(_F32)
    wq = jnp.tile(q_s.reshape(nseq, dec, a), (1, ATTN_HEADS, 1)) * head_mask[None]
    pad_new = lambda z: jnp.pad(z.reshape(nseq, dec, a), ((0, 0), (0, page - dec), (0, 0))).astype(_BF)
    slope_rows = jnp.asarray(_slopes(), _F32)[head_of_row].reshape(nq, 1)
    qpos_rows = (past + jnp.arange(nq) % dec).astype(_F32).reshape(nq, 1)
    token_minor = lambda c: jnp.transpose(c[0], (0, 2, 3, 1)).reshape(n_phys, a, page)
    pg = 32 if n_pages % 32 == 0 else blk // page
    attn_s = _moba_sample(page_table, wq, pad_new(k_s), pad_new(v_s), slope_rows, qpos_rows,
                          token_minor(cache_k), token_minor(cache_v), pg=pg, past=past)

    to_groups = lambda z: jnp.transpose(z.reshape(nseq // spg, spg, dec, -1), (0, 2, 1, 3))
    u_g = to_groups(u_s).reshape(nseq // spg, dec * spg, sw)
    h0r = state_ssm_re[0].reshape(nseq // spg, spg, ns)
    h0i = state_ssm_im[0].reshape(nseq // spg, spg, ns)
    ssm_g, hr_s, hi_s = _s5(u_g, h0r, h0i, *s5_w, tc=dec)
    ssm_s = jnp.transpose(ssm_g.reshape(nseq // spg, dec, spg, sw), (0, 2, 1, 3)).reshape(n_s, sw)

    nts = n_s // tms
    x1_s, cq_s = _mix(x_sample.reshape(n_s, d), attn_s.reshape(n_s, a), ssm_s, lambda bi, ti: (ti, 0),
                      wo_a, wo_s, row2(g_cross[0]), w_cq_b, nb=1, nt=nts, tm=tms)
    mem_rows = lambda c: c[0].reshape(nseq, mtok * MEM_HEADS, MEM_HEAD_DIM)
    o_s = _cross_attn(cq_s, mem_rows(cache_mem_k), mem_rows(cache_mem_v), sb=spg, rq=dec)
    y_s = _mlp(x1_s, o_s, w_co_b, row2(g_mlp[0]), w1_b, w2_b, row2(g_final), tm=tms)

    kv_out = lambda z: jnp.transpose(z.reshape((1, b) + hd + (t,)), (0, 1, 4, 2, 3))
    return (y_p.reshape(b, t, d), y_s.reshape(nseq, dec, d),
            kv_out(kt_p), kv_out(vt_p),
            hr_p.reshape(1, b, groups, ns_p), hi_p.reshape(1, b, groups, ns_p),
            mk_p.reshape((1,) + mem4), mv_p.reshape((1,) + mem4),
            k_s.reshape((1, nseq, dec) + hd), v_s.reshape((1, nseq, dec) + hd),
            hr_s.reshape(1, nseq, groups, ns_p), hi_s.reshape(1, nseq, groups, ns_p))
```

```python
import functools

import jax
import jax.numpy as jnp
from jax import lax
from jax.experimental import pallas as pl
from jax.experimental.pallas import tpu as pltpu

ATTN_HEADS = 8
HEAD_DIM = 64
ATTN_WIDTH = ATTN_HEADS * HEAD_DIM
MOBA_BLOCK = 256
MOBA_TOPK = 3
SSM_GROUP = 16
MEM_HEADS = 4
MEM_HEAD_DIM = 128
RMS_EPS = 1e-6
SEQS_PER_GROUP = 8
LANES = 128
MXU_N = 256
VMEM_LIMIT = 56 * 1024 * 1024

_BF = jnp.bfloat16
_F32 = jnp.float32
_NEG = -1e30
_BIG = 1e30
_LOG2E = 1.4426950408889634
_NT = (((1,), (1,)), ((), ()))


def _slopes():
    return [2.0 ** (-8.0 * (h + 1) / ATTN_HEADS) for h in range(ATTN_HEADS)]


def _rms(x, g):
    return x * lax.rsqrt(jnp.mean(x * x, axis=-1, keepdims=True) + RMS_EPS) * g


def _dot(a, b):
    return jnp.dot(a, b, preferred_element_type=_F32)


def _dot_nt(a, b):
    return lax.dot_general(a, b, _NT, preferred_element_type=_F32)


def _split_bf16(x):
    hi = x.astype(_BF)
    lo = (x - hi.astype(_F32)).astype(_BF)
    return hi, lo


def _dot_nt_3pass(a, b):
    a_hi, a_lo = _split_bf16(a)
    b_hi, b_lo = _split_bf16(b)
    return _dot_nt(a_hi, b_hi) + _dot_nt(a_hi, b_lo) + _dot_nt(a_lo, b_hi)


def _rank_update(rank, gm, g, tie):
    return rank + jnp.where(gm > g, 1.0, jnp.where(gm == g, tie, 0.0))


def _params(sem):
    return pltpu.CompilerParams(dimension_semantics=sem, vmem_limit_bytes=VMEM_LIMIT)


def _in_proj_prompt_kernel(x_ref, g_ref, w_ref, q_ref, kt_ref, vt_ref, kb_ref, vtb_ref, u_ref, km_ref):
    a = ATTN_WIDTH
    hn = _rms(x_ref[...], g_ref[...]).astype(_BF)
    q_ref[...] = _dot(hn, w_ref[:, 0:a])
    k = _dot(hn, w_ref[:, a:2 * a])
    v = _dot(hn, w_ref[:, 2 * a:3 * a])
    kt_ref[...] = k.T
    vt = v.T
    vt_ref[...] = vt
    kb_ref[...] = k.astype(_BF)
    u_ref[...] = _dot(hn, w_ref[:, 3 * a:])
    for r in range(k.shape[0] // MOBA_BLOCK):
        rows = slice(r * MOBA_BLOCK, (r + 1) * MOBA_BLOCK)
        vtb_ref[r] = vt[:, rows].astype(_BF)
        km_ref[r] = jnp.mean(k[rows, :], axis=0, keepdims=True)


def _in_proj_prompt(x, g, w, *, tm):
    b, t, d = x.shape
    a = ATTN_WIDTH
    su = w.shape[1] - 3 * a
    nt = t // tm
    nbt = tm // MOBA_BLOCK
    nbg = b // SEQS_PER_GROUP
    row = lambda bi, ti: (bi * nt + ti, 0)
    tok_minor = lambda bi, ti: (bi, 0, ti)
    return pl.pallas_call(
        _in_proj_prompt_kernel,
        grid=(b, nt),
        in_specs=[pl.BlockSpec((tm, d), row),
                  pl.BlockSpec((1, d), lambda bi, ti: (0, 0)),
                  pl.BlockSpec(w.shape, lambda bi, ti: (0, 0))],
        out_specs=[pl.BlockSpec((tm, a), row),
                   pl.BlockSpec((None, a, tm), tok_minor),
                   pl.BlockSpec((None, a, tm), tok_minor),
                   pl.BlockSpec((tm, a), row),
                   pl.BlockSpec((None, nbt, a, MOBA_BLOCK), lambda bi, ti: (bi, ti, 0, 0)),
                   pl.BlockSpec((None, tm, su), lambda bi, ti: (bi // SEQS_PER_GROUP, ti, bi % SEQS_PER_GROUP)),
                   pl.BlockSpec((nbt, 1, a), lambda bi, ti: (bi * nt + ti, 0, 0))],
        out_shape=[jax.ShapeDtypeStruct((b * t, a), _F32),
                   jax.ShapeDtypeStruct((b, a, t), _F32),
                   jax.ShapeDtypeStruct((b, a, t), _F32),
                   jax.ShapeDtypeStruct((b * t, a), _BF),
                   jax.ShapeDtypeStruct((b, t // MOBA_BLOCK, a, MOBA_BLOCK), _BF),
                   jax.ShapeDtypeStruct((nbg, t, SEQS_PER_GROUP * su), _F32),
                   jax.ShapeDtypeStruct((b * t // MOBA_BLOCK, 1, a), _F32)],
        compiler_params=_params(("parallel", "parallel")),
        name="in_proj_prompt",
    )(x.reshape(b * t, d), g, w)


def _norm_proj_kernel(x_ref, g_ref, w_ref, *o_refs):
    hn = _rms(x_ref[...], g_ref[...]).astype(_BF)
    c0 = 0
    for o_ref in o_refs:
        c1 = c0 + o_ref.shape[1]
        o_ref[...] = _dot(hn, w_ref[:, c0:c1])
        c0 = c1


def _norm_proj(x, g, w, widths, *, tm, name):
    n, d = x.shape
    row = lambda i: (i, 0)
    return pl.pallas_call(
        _norm_proj_kernel,
        grid=(n // tm,),
        in_specs=[pl.BlockSpec((tm, d), row),
                  pl.BlockSpec((1, d), lambda i: (0, 0)),
                  pl.BlockSpec(w.shape, lambda i: (0, 0))],
        out_specs=[pl.BlockSpec((tm, c), row) for c in widths],
        out_shape=[jax.ShapeDtypeStruct((n, c), _F32) for c in widths],
        compiler_params=_params(("parallel",)),
        name=name,
    )(x, g, w)


def _moba_prompt_kernel(q_ref, k_ref, vt_ref, km_ref, o_ref,
                        sel_ref, bias_ref, qx_ref, sa_ref, sb_ref, m_ref, l_ref, acc_ref, *, nb, nbp):
    blk = MOBA_BLOCK
    nh = ATTN_HEADS
    i = pl.program_id(1)
    q = q_ref[...]
    slopes2 = [s * _LOG2E for s in _slopes()]
    kk = lax.broadcasted_iota(jnp.int32, (blk, blk), 0)
    qq = lax.broadcasted_iota(jnp.int32, (blk, blk), 1)
    dkq = (kk - qq).astype(_F32)

    @pl.when(i == 0)
    def _fill_bias():
        for h in range(nh):
            bias_ref[h] = slopes2[h] * dkq

    gt = _dot_nt_3pass(km_ref[...], q)
    rows = lax.broadcasted_iota(jnp.int32, (nbp, blk), 0)
    fully_past = rows < i
    g = [jnp.where(fully_past, gt[h * nbp:(h + 1) * nbp, :], _NEG) for h in range(nh)]
    rank = [jnp.zeros((nbp, blk), _F32) for _ in range(nh)]
    for m in range(nb):
        tie = jnp.where(rows > m, 1.0, 0.0)
        for h in range(nh):
            rank[h] = _rank_update(rank[h], g[h][m:m + 1, :], g[h], tie)
    for h in range(nh):
        sel_ref[h * nbp:(h + 1) * nbp, :] = jnp.where(fully_past & (rank[h] < MOBA_TOPK), 1.0, 0.0)

    qs = (q * (HEAD_DIM ** -0.5 * _LOG2E)).astype(_BF)
    lane = lax.broadcasted_iota(jnp.int32, (blk, LANES), 1)
    zero = jnp.zeros((), _BF)
    for p in range(nh // 2):
        qp = qs[:, p * LANES:(p + 1) * LANES]
        qx_ref[p, 0:blk, :] = jnp.where(lane < HEAD_DIM, qp, zero)
        qx_ref[p, blk:2 * blk, :] = jnp.where(lane >= HEAD_DIM, qp, zero)

    causal_neg = jnp.where(dkq <= 0.0, 0.0, _NEG)
    ones_rows = jnp.ones((8, blk), _BF)

    def scores(j, s_out, p):
        start = pl.multiple_of(j * blk, blk)
        kb = k_ref[pl.ds(start, blk), p * LANES:(p + 1) * LANES]
        s_out[p] = _dot_nt(kb, qx_ref[p])

    def step(j_next, s_next, j, s_cur, own):
        coff = None if own else ((i - j) * blk).astype(_F32)
        for p in range(nh // 2):
            if j_next is not None:
                scores(j_next, s_next, p)
            vb = vt_ref[j, p * LANES:(p + 1) * LANES, :]
            ptbs = []
            alphas = []
            for x in range(2):
                h = 2 * p + x
                s = s_cur[p, :, x * blk:(x + 1) * blk] + bias_ref[h]
                if own:
                    s = s + causal_neg
                    m_new = jnp.max(s, axis=0, keepdims=True)
                    t = m_new
                    alphas.append(None)
                else:
                    c = slopes2[h] * coff
                    picked = sel_ref[pl.ds(h * nbp + j, 1), :] > 0.5
                    m_old = m_ref[h]
                    m_new = jnp.where(picked, jnp.maximum(m_old, jnp.max(s, axis=0, keepdims=True) - c), m_old)
                    t = jnp.where(picked, m_new + c, _BIG)
                    alphas.append(jnp.exp2(m_old - m_new))
                m_ref[h] = m_new
                ptbs.append(jnp.exp2(s - t).astype(_BF))
            for x in range(2):
                h = 2 * p + x
                arow = slice(h * HEAD_DIM, (h + 1) * HEAD_DIM)
                psum = _dot(ones_rows, ptbs[x])[0:1, :]
                pv = _dot(vb[x * HEAD_DIM:(x + 1) * HEAD_DIM, :], ptbs[x])
                if own:
                    l_ref[h] = psum
                    acc_ref[arow, :] = pv
                else:
                    l_ref[h] = alphas[x] * l_ref[h] + psum
                    acc_ref[arow, :] = acc_ref[arow, :] * alphas[x] + pv

    for p in range(nh // 2):
        scores(i, sa_ref, p)
    step(0, sb_ref, i, sa_ref, True)
    last = jnp.maximum(i - 1, 0)

    def pair(mi, carry):
        b0 = 2 * mi
        step(b0 + 1, sa_ref, b0, sb_ref, False)
        step(jnp.minimum(b0 + 2, last), sb_ref, b0 + 1, sa_ref, False)
        return carry

    lax.fori_loop(0, i // 2, pair, 0)

    @pl.when(i % 2 == 1)
    def _odd_tail():
        step(None, None, i - 1, sb_ref, False)

    for h in range(nh):
        arow = slice(h * HEAD_DIM, (h + 1) * HEAD_DIM)
        o_ref[arow, :] = acc_ref[arow, :] * (1.0 / l_ref[h])


def _moba_prompt(q, kb, vt, kmblk):
    b, t, a = kb.shape
    blk = MOBA_BLOCK
    nb = t // blk
    nbp = kmblk.shape[1] // ATTN_HEADS
    return pl.pallas_call(
        functools.partial(_moba_prompt_kernel, nb=nb, nbp=nbp),
        grid=(b, nb),
        in_specs=[pl.BlockSpec((blk, a), lambda bi, i: (bi * nb + i, 0)),
                  pl.BlockSpec((None, t, a), lambda bi, i: (bi, 0, 0)),
                  pl.BlockSpec((None, nb, a, blk), lambda bi, i: (bi, 0, 0, 0)),
                  pl.BlockSpec((None,) + kmblk.shape[1:], lambda bi, i: (bi, 0, 0))],
        out_specs=pl.BlockSpec((None, a, blk), lambda bi, i: (bi, 0, i)),
        out_shape=jax.ShapeDtypeStruct((b, a, t), _F32),
        scratch_shapes=[pltpu.VMEM((ATTN_HEADS * nbp, blk), _F32),
                        pltpu.VMEM((ATTN_HEADS, blk, blk), _F32),
                        pltpu.VMEM((ATTN_HEADS // 2, 2 * blk, LANES), _BF),
                        pltpu.VMEM((ATTN_HEADS // 2, blk, 2 * blk), _F32),
                        pltpu.VMEM((ATTN_HEADS // 2, blk, 2 * blk), _F32),
                        pltpu.VMEM((ATTN_HEADS, 1, blk), _F32),
                        pltpu.VMEM((ATTN_HEADS, 1, blk), _F32),
                        pltpu.VMEM((a, blk), _F32)],
        compiler_params=_params(("parallel", "arbitrary")),
        name="moba_prompt",
    )(q, kb, vt, kmblk)


def _moba_sample_kernel(pt_ref, wq_ref, kn_ref, vn_ref, slope_ref, qpos_ref, *rest,
                        pg, n_pages, page, past, dec):
    del pt_ref
    k_refs = rest[:pg]
    v_refs = rest[pg:2 * pg]
    o_ref, s_ref, gate_ref, acc_ref, l_ref = rest[2 * pg:]
    blk = MOBA_BLOCK
    ppb = blk // page
    nb = n_pages // ppb
    n_groups = n_pages // pg
    step = pl.program_id(1)
    wq = wq_ref[...]
    nq = wq.shape[0]
    wqb = (wq * (HEAD_DIM ** -0.5 * _LOG2E)).astype(_BF)
    blk_lane = lax.broadcasted_iota(jnp.int32, (nq, LANES), 1)

    @pl.when(step == 0)
    def _init():
        gate_ref[...] = jnp.zeros(gate_ref.shape, _F32)

    @pl.when(step < n_groups)
    def _k_phase():
        for c in range(pg // ppb):
            gs = jnp.zeros((nq, page), _F32)
            for r in range(c * ppb, (c + 1) * ppb):
                st = _dot(wqb, k_refs[r][...].astype(_BF))
                s_ref[step * pg + r] = st
                gs = gs + st
            n = step * (pg // ppb) + c
            gsum = jnp.sum(gs, axis=1, keepdims=True)
            gate_ref[...] = jnp.where(blk_lane == n, gsum, gate_ref[...])

    @pl.when(step == n_groups - 1)
    def _softmax():
        gate = gate_ref[...]
        rank = jnp.zeros(gate.shape, _F32)
        for m in range(nb):
            tie = jnp.where(blk_lane > m, 1.0, 0.0)
            rank = _rank_update(rank, gate[:, m:m + 1], gate, tie)
        drop = jnp.where(rank < MOBA_TOPK, 0.0, _NEG)
        lane = lax.broadcasted_iota(jnp.int32, (nq, page), 1).astype(_F32)
        slope2 = jnp.broadcast_to(slope_ref[...] * _LOG2E, (nq, page))
        qpos = jnp.broadcast_to(qpos_ref[...], (nq, page))
        base = slope2 * (lane - qpos)

        mm = jnp.full((nq, page), _NEG, _F32)
        for p in range(n_pages):
            n = p // ppb
            z = s_ref[p] + (base + slope2 * float(p * page)) + drop[:, n:n + 1]
            s_ref[p] = z
            mm = jnp.maximum(mm, z)
        so = _dot_nt(wqb, kn_ref[...]) + (base + slope2 * float(past))
        so = jnp.where(lane + float(past) <= qpos, so, _NEG)
        m = jnp.max(jnp.maximum(mm, so), axis=1, keepdims=True)
        ls = jnp.exp2(so - m)
        acc_ref[...] = _dot(ls.astype(_BF), vn_ref[...])
        for p in range(n_pages):
            pp = jnp.exp2(s_ref[p] - m)
            s_ref[p] = pp
            ls = ls + pp
        l_ref[...] = jnp.sum(ls, axis=1, keepdims=True)

    @pl.when(step >= n_groups)
    def _v_phase():
        acc = acc_ref[...]
        for r in range(pg):
            pp = s_ref[(step - n_groups) * pg + r]
            acc = acc + _dot_nt(pp.astype(_BF), v_refs[r][...].astype(_BF))
        acc_ref[...] = acc

    @pl.when(step == 2 * n_groups - 1)
    def _finish():
        o = acc_ref[...] * (1.0 / l_ref[...])
        lane_a = lax.broadcasted_iota(jnp.int32, (dec, o.shape[1]), 1)
        out = jnp.zeros((dec, o.shape[1]), _F32)
        for h in range(ATTN_HEADS):
            in_head = (lane_a >= h * HEAD_DIM) & (lane_a < (h + 1) * HEAD_DIM)
            out = out + jnp.where(in_head, o[h * dec:(h + 1) * dec, :], 0.0)
        o_ref[...] = out


def _moba_sample(page_table, wq, kn, vn, slope_rows, qpos_rows, cache_kt, cache_vt, *, pg, past):
    nseq, n_pages = page_table.shape
    _, a, page = cache_kt.shape
    nq = wq.shape[1]
    dec = nq // ATTN_HEADS
    n_groups = n_pages // pg
    assert n_pages * page // MOBA_BLOCK <= LANES
    seq = lambda bi, s, pt: (bi, 0, 0)
    const = lambda bi, s, pt: (0, 0)

    def k_map(r):
        return lambda bi, s, pt: (pt[bi, jnp.minimum(s, n_groups - 1) * pg + r], 0, 0)

    def v_map(r):
        def index(bi, s, pt):
            in_v = s >= n_groups
            seq_i = jnp.where(in_v, bi, jnp.maximum(bi - 1, 0))
            grp = jnp.where(in_v, s - n_groups, n_groups - 1)
            return (pt[seq_i, grp * pg + r], 0, 0)
        return index

    in_specs = [pl.BlockSpec((None, nq, a), seq),
                pl.BlockSpec((None, page, a), seq),
                pl.BlockSpec((None, page, a), seq),
                pl.BlockSpec((nq, 1), const),
                pl.BlockSpec((nq, 1), const)]
    in_specs += [pl.BlockSpec((None, a, page), k_map(r)) for r in range(pg)]
    in_specs += [pl.BlockSpec((None, a, page), v_map(r)) for r in range(pg)]
    return pl.pallas_call(
        functools.partial(_moba_sample_kernel, pg=pg, n_pages=n_pages, page=page, past=past, dec=dec),
        grid_spec=pltpu.PrefetchScalarGridSpec(
            num_scalar_prefetch=1,
            grid=(nseq, 2 * n_groups),
            in_specs=in_specs,
            out_specs=pl.BlockSpec((None, dec, a), seq),
            scratch_shapes=[pltpu.VMEM((n_pages, nq, page), _F32),
                            pltpu.VMEM((nq, LANES), _F32),
                            pltpu.VMEM((nq, a), _F32),
                            pltpu.VMEM((nq, 1), _F32)]),
        out_shape=jax.ShapeDtypeStruct((nseq, dec, a), _F32),
        compiler_params=_params(("parallel", "arbitrary")),
        name="moba_sample",
    )(page_table, wq, kn, vn, slope_rows, qpos_rows, *([cache_kt] * pg), *([cache_vt] * pg))


def _s5_disc_kernel(are_ref, aim_ref, ldt_ref, bre_ref, bim_ref, lbr_ref, lbi_ref, bbr_ref, bbi_ref):
    are = are_ref[...]
    aim = aim_ref[...]
    dt = jnp.exp(ldt_ref[...])
    mag = jnp.exp(are * dt)
    ang = aim * dt
    lb_re = mag * jnp.cos(ang)
    lb_im = mag * jnp.sin(ang)
    e_re = lb_re - 1.0
    e_im = lb_im
    den = are * are + aim * aim
    f_re = (e_re * are + e_im * aim) / den
    f_im = (e_im * are - e_re * aim) / den
    br = bre_ref[...]
    bi = bim_ref[...]
    lbr_ref[...] = lb_re
    lbi_ref[...] = lb_im
    bbr_ref[...] = f_re * br - f_im * bi
    bbi_ref[...] = f_re * bi + f_im * br


def _s5_disc(a_re, a_im, log_dt, b_re, b_im):
    g, p, c = b_re.shape
    rep = lambda z: jnp.repeat(z, c, axis=0)
    are, aim = rep(a_re), rep(a_im)
    ldt = rep(log_dt.reshape(g, 1))
    b2 = lambda z: jnp.transpose(z, (0, 2, 1)).reshape(g * c, p)
    shp = jax.ShapeDtypeStruct((g * c, p), _F32)
    lbr, lbi, bbr, bbi = pl.pallas_call(
        _s5_disc_kernel, out_shape=[shp] * 4, name="s5_disc",
    )(are, aim, ldt, b2(b_re), b2(b_im))
    return lbr[::c], lbi[::c], bbr, bbi


def _s5_kernel(u_ref, pm_ref, pmt_ref, h0r_ref, h0i_ref, lbr_ref, lbi_ref, bbr_ref, bbi_ref, cr_ref, ci_ref, d_ref,
               wg_ref, bg_ref, y_ref, hr_out, hi_out, hre, him, st_r, st_i, *, tc, cw, seq_lanes):
    spg = SEQS_PER_GROUP
    t = pl.program_id(1)
    ns = hre.shape[1]
    w = wg_ref.shape[0]
    if seq_lanes:
        u_st = jnp.concatenate([u_ref[:, b * w:(b + 1) * w] for b in range(spg)], axis=0)
        u_hi, u_lo = _split_bf16(u_st)
        u = _dot(pm_ref[...], u_hi) + _dot(pm_ref[...], u_lo)
    else:
        u = u_ref[...]
    ub = u.astype(_BF)
    ratio = ns // w
    for n in range(ns // MXU_N):
        out = slice(n * MXU_N, (n + 1) * MXU_N)
        lo = (n * MXU_N // ratio) // LANES * LANES
        src = slice(lo, lo + LANES)
        hre[:, out] = _dot(ub[:, src], bbr_ref[src, out])
        him[:, out] = _dot(ub[:, src], bbi_ref[src, out])

    @pl.when(t == 0)
    def _init():
        st_r[...] = h0r_ref[...]
        st_i[...] = h0i_ref[...]

    for c in range(ns // cw):
        cs = slice(c * cw, (c + 1) * cw)
        lr = jnp.broadcast_to(lbr_ref[:, cs], (spg, cw))
        li = jnp.broadcast_to(lbi_ref[:, cs], (spg, cw))

        def step(tt, carry):
            sr, si = carry
            r0 = pl.multiple_of(tt * spg, spg)
            nr = lr * sr - li * si + hre[pl.ds(r0, spg), cs]
            ni = lr * si + li * sr + him[pl.ds(r0, spg), cs]
            hre[pl.ds(r0, spg), cs] = nr
            him[pl.ds(r0, spg), cs] = ni
            return nr, ni

        sr, si = lax.fori_loop(0, tc, step, (st_r[:, cs], st_i[:, cs]))
        st_r[:, cs] = sr
        st_i[:, cs] = si

    parts = []
    for m in range(w // MXU_N):
        out = slice(m * MXU_N, (m + 1) * MXU_N)
        src = slice(m * MXU_N * ratio, (m + 1) * MXU_N * ratio)
        parts.append(_dot(hre[:, src].astype(_BF), cr_ref[src, out])
                     - _dot(him[:, src].astype(_BF), ci_ref[src, out]))
    y = jnp.concatenate(parts, axis=1) + d_ref[...] * u
    z = _dot(y.astype(_BF), wg_ref[...]) + bg_ref[...]
    y = y * (1.0 / (1.0 + jnp.exp(-z)))
    if seq_lanes:
        y_st = _dot(pmt_ref[...], y.astype(_BF)).astype(y_ref.dtype)
        for b in range(spg):
            y_ref[:, b * w:(b + 1) * w] = y_st[b * tc:(b + 1) * tc, :]
    else:
        y_ref[...] = y.astype(y_ref.dtype)
    hr_out[...] = st_r[...]
    hi_out[...] = st_i[...]


def _s5(u, h0r, h0i, lbr, lbi, bbr, bbi, cr, ci, d, wg, bg, *, tc, seq_lanes):
    spg = SEQS_PER_GROUP
    w = wg.shape[0]
    nbg = u.shape[0]
    t = u.shape[1] if seq_lanes else u.shape[1] // spg
    ns = lbr.shape[1]
    ratio = ns // w
    assert ns % MXU_N == 0 and w % MXU_N == 0 and LANES % (MXU_N // ratio) == 0
    const = lambda gi, ti: (0, 0)
    grp = lambda gi, ti: (gi, 0, 0)
    io_block = (None, tc, spg * w) if seq_lanes else (None, tc * spg, w)
    r = tc * spg
    dst = jnp.arange(r)
    pm = (dst[:, None] % spg * tc + dst[:, None] // spg == dst[None, :]).astype(_BF)
    return pl.pallas_call(
        functools.partial(_s5_kernel, tc=tc, cw=min(ns, 4 * LANES), seq_lanes=seq_lanes),
        grid=(nbg, t // tc),
        in_specs=[pl.BlockSpec(io_block, lambda gi, ti: (gi, ti, 0)),
                  pl.BlockSpec((r, r), const), pl.BlockSpec((r, r), const),
                  pl.BlockSpec((None, spg, ns), grp),
                  pl.BlockSpec((None, spg, ns), grp),
                  pl.BlockSpec((1, ns), const), pl.BlockSpec((1, ns), const),
                  pl.BlockSpec((w, ns), const), pl.BlockSpec((w, ns), const),
                  pl.BlockSpec((ns, w), const), pl.BlockSpec((ns, w), const),
                  pl.BlockSpec((1, w), const),
                  pl.BlockSpec((w, w), const), pl.BlockSpec((1, w), const)],
        out_specs=[pl.BlockSpec(io_block, lambda gi, ti: (gi, ti, 0)),
                   pl.BlockSpec((None, spg, ns), grp),
                   pl.BlockSpec((None, spg, ns), grp)],
        out_shape=[jax.ShapeDtypeStruct(u.shape, _BF if seq_lanes else _F32),
                   jax.ShapeDtypeStruct((nbg, spg, ns), _F32),
                   jax.ShapeDtypeStruct((nbg, spg, ns), _F32)],
        scratch_shapes=[pltpu.VMEM((r, ns), _F32), pltpu.VMEM((r, ns), _F32),
                        pltpu.VMEM((spg, ns), _F32), pltpu.VMEM((spg, ns), _F32)],
        compiler_params=_params(("parallel", "arbitrary")),
        name="s5",
    )(u, pm, pm.T, h0r, h0i, lbr, lbi, bbr, bbi, cr, ci, d, wg, bg)


def _block_diag(blocks):
    g, r, c = blocks.shape
    eye = jnp.eye(g, dtype=blocks.dtype)
    return (blocks[:, :, None, :] * eye[:, None, :, None]).reshape(g * r, g * c)


def _mix_kernel(x_ref, a_ref, s_ref, woa_ref, wos_ref, gc_ref, wcq_ref, x1_ref, cq_ref):
    x1 = (x_ref[...] + _dot(a_ref[...].astype(_BF), woa_ref[...])
          + _dot(s_ref[...].astype(_BF), wos_ref[...]))
    x1_ref[...] = x1
    cq_ref[...] = _dot(_rms(x1, gc_ref[...]).astype(_BF), wcq_ref[...])


def _mix(x, attn, ssm, ssm_map, woa, wos, gc, wcq, *, nb, nt, tm):
    n, d = x.shape
    row = lambda bi, ti: (bi * nt + ti, 0)
    const = lambda bi, ti: (0, 0)
    mw = wcq.shape[1]
    sw = wos.shape[0]
    return pl.pallas_call(
        _mix_kernel,
        grid=(nb, nt),
        in_specs=[pl.BlockSpec((tm, d), row),
                  pl.BlockSpec((tm, attn.shape[1]), row),
                  pl.BlockSpec((tm, sw), ssm_map),
                  pl.BlockSpec(woa.shape, const), pl.BlockSpec(wos.shape, const),
                  pl.BlockSpec((1, d), const), pl.BlockSpec(wcq.shape, const)],
        out_specs=[pl.BlockSpec((tm, d), row), pl.BlockSpec((tm, mw), row)],
        out_shape=[jax.ShapeDtypeStruct((n, d), _F32), jax.ShapeDtypeStruct((n, mw), _F32)],
        compiler_params=_params(("parallel", "parallel")),
        name="mix",
    )(x, attn, ssm, woa, wos, gc, wcq)


def _mem_kv_kernel(x_ref, g_ref, w_ref, mk_ref, mv_ref, mkb_ref, mvb_ref):
    hn = _rms(x_ref[...], g_ref[...]).astype(_BF)
    mw = MEM_HEADS * MEM_HEAD_DIM
    k = _dot(hn, w_ref[:, :mw])
    v = _dot(hn, w_ref[:, mw:])
    mkb_ref[...] = k.astype(_BF)
    mvb_ref[...] = v.astype(_BF)
    for h in range(MEM_HEADS):
        cols = slice(h * MEM_HEAD_DIM, (h + 1) * MEM_HEAD_DIM)
        mk_ref[:, h, :] = k[:, cols]
        mv_ref[:, h, :] = v[:, cols]


def _mem_kv(mem, g, w, *, tm):
    n, d = mem.shape
    mw = MEM_HEADS * MEM_HEAD_DIM
    shp = jax.ShapeDtypeStruct((n, MEM_HEADS, MEM_HEAD_DIM), _F32)
    spec = pl.BlockSpec((tm, MEM_HEADS, MEM_HEAD_DIM), lambda i: (i, 0, 0))
    flat = pl.BlockSpec((tm, mw), lambda i: (i, 0))
    return pl.pallas_call(
        _mem_kv_kernel,
        grid=(n // tm,),
        in_specs=[pl.BlockSpec((tm, d), lambda i: (i, 0)),
                  pl.BlockSpec((1, d), lambda i: (0, 0)),
                  pl.BlockSpec(w.shape, lambda i: (0, 0))],
        out_specs=[spec, spec, flat, flat],
        out_shape=[shp, shp, jax.ShapeDtypeStruct((n, mw), _BF), jax.ShapeDtypeStruct((n, mw), _BF)],
        compiler_params=_params(("parallel",)),
        name="mem_kv",
    )(mem, g, w)


def _cross_attn_kernel(q_ref, mk_ref, mv_ref, o_ref, *, sb, rq):
    nrow = MEM_HEADS * rq
    nmem = mk_ref.shape[1]
    row_head = lax.broadcasted_iota(jnp.int32, (nrow, nmem), 0) // rq
    col_head = lax.broadcasted_iota(jnp.int32, (nrow, nmem), 1) % MEM_HEADS
    same_head = row_head == col_head
    for s in range(sb):
        q = q_ref[s * rq:(s + 1) * rq, :]
        wq = jnp.concatenate([q[:, h * MEM_HEAD_DIM:(h + 1) * MEM_HEAD_DIM] for h in range(MEM_HEADS)], axis=0)
        sc = _dot_nt(wq.astype(_BF), mk_ref[s].astype(_BF)) * (MEM_HEAD_DIM ** -0.5)
        sc = jnp.where(same_head, sc, _NEG)
        m = jnp.max(sc, axis=-1, keepdims=True)
        e = jnp.exp(sc - m)
        p = e * (1.0 / jnp.sum(e, axis=-1, keepdims=True))
        o = _dot(p.astype(_BF), mv_ref[s].astype(_BF))
        o_ref[s * rq:(s + 1) * rq, :] = jnp.concatenate(
            [o[h * rq:(h + 1) * rq, :] for h in range(MEM_HEADS)], axis=1)


def _cross_attn(cq, mk, mv, *, sb, rq):
    n, mw = cq.shape
    nseq, nmem, dh = mk.shape
    mem = lambda si: (si, 0, 0)
    return pl.pallas_call(
        functools.partial(_cross_attn_kernel, sb=sb, rq=rq),
        grid=(nseq // sb,),
        in_specs=[pl.BlockSpec((sb * rq, mw), lambda si: (si, 0)),
                  pl.BlockSpec((sb, nmem, dh), mem),
                  pl.BlockSpec((sb, nmem, dh), mem)],
        out_specs=pl.BlockSpec((sb * rq, mw), lambda si: (si, 0)),
        out_shape=jax.ShapeDtypeStruct((n, mw), _F32),
        compiler_params=_params(("parallel",)),
        name="cross_attn",
    )(cq, mk, mv)


def _mlp_tail(x2, gm_ref, w1_ref, w2_ref, gf_ref, fc):
    hn = _rms(x2, gm_ref[...]).astype(_BF)
    acc = x2
    for c in range(w1_ref.shape[1] // fc):
        cols = slice(c * fc, (c + 1) * fc)
        hid = jnp.square(jnp.maximum(_dot(hn, w1_ref[:, cols]), 0.0)).astype(_BF)
        acc = acc + _dot(hid, w2_ref[cols, :])
    return _rms(acc, gf_ref[...])


def _mlp_kernel(x1_ref, o_ref, wco_ref, gm_ref, w1_ref, w2_ref, gf_ref, y_ref, *, fc):
    x2 = x1_ref[...] + _dot(o_ref[...].astype(_BF), wco_ref[...])
    y_ref[...] = _mlp_tail(x2, gm_ref, w1_ref, w2_ref, gf_ref, fc)


def _post_kernel(x_ref, a_ref, s_ref, mk_ref, mv_ref, woa_ref, wos_ref, gc_ref, wcq_ref, wco_ref,
                 gm_ref, w1_ref, w2_ref, gf_ref, y_ref, *, fc):
    x1 = (x_ref[...] + _dot(a_ref[...].T.astype(_BF), woa_ref[...])
          + _dot(s_ref[...].astype(_BF), wos_ref[...]))
    cq = _dot(_rms(x1, gc_ref[...]).astype(_BF), wcq_ref[...]).astype(_BF)
    heads = []
    for h in range(MEM_HEADS):
        cols = slice(h * MEM_HEAD_DIM, (h + 1) * MEM_HEAD_DIM)
        sc = _dot_nt(cq[:, cols], mk_ref[:, cols]) * (MEM_HEAD_DIM ** -0.5)
        m = jnp.max(sc, axis=-1, keepdims=True)
        e = jnp.exp(sc - m)
        p = e * (1.0 / jnp.sum(e, axis=-1, keepdims=True))
        heads.append(_dot(p.astype(_BF), mv_ref[:, cols]))
    o = jnp.concatenate(heads, axis=1)
    x2 = x1 + _dot(o.astype(_BF), wco_ref[...])
    y_ref[...] = _mlp_tail(x2, gm_ref, w1_ref, w2_ref, gf_ref, fc)


def _post(x, attn, ssm, ssm_map, mkb, mvb, woa, wos, gc, wcq, wco, gm, w1, w2, gf, *, nb, nt, tm):
    n, d = x.shape
    row = lambda bi, ti: (bi * nt + ti, 0)
    const = lambda bi, ti: (0, 0)
    mem = lambda bi, ti: (bi, 0, 0)
    resident = lambda z: pl.BlockSpec(z.shape, const, pipeline_mode=pl.Buffered(1))
    return pl.pallas_call(
        functools.partial(_post_kernel, fc=min(w1.shape[1], 1024)),
        grid=(nb, nt),
        in_specs=[pl.BlockSpec((tm, d), row),
                  pl.BlockSpec((None, attn.shape[1], tm), lambda bi, ti: (bi, 0, ti)),
                  pl.BlockSpec((tm, wos.shape[0]), ssm_map),
                  pl.BlockSpec((None,) + mkb.shape[1:], mem), pl.BlockSpec((None,) + mvb.shape[1:], mem),
                  resident(woa), resident(wos), resident(gc), resident(wcq), resident(wco),
                  resident(gm), resident(w1), resident(w2), resident(gf)],
        out_specs=pl.BlockSpec((tm, d), row),
        out_shape=jax.ShapeDtypeStruct((n, d), _F32),
        compiler_params=_params(("parallel", "parallel")),
        name="post",
    )(x, attn, ssm, mkb, mvb, woa, wos, gc, wcq, wco, gm, w1, w2, gf)


def _mlp(x1, o, wco, gm, w1, w2, gf, *, tm):
    n, d = x1.shape
    row = lambda i: (i, 0)
    const = lambda i: (0, 0)
    return pl.pallas_call(
        functools.partial(_mlp_kernel, fc=min(w1.shape[1], 1024)),
        grid=(n // tm,),
        in_specs=[pl.BlockSpec((tm, d), row), pl.BlockSpec((tm, o.shape[1]), row),
                  pl.BlockSpec(wco.shape, const), pl.BlockSpec((1, d), const),
                  pl.BlockSpec(w1.shape, const), pl.BlockSpec(w2.shape, const),
                  pl.BlockSpec((1, d), const)],
        out_specs=pl.BlockSpec((tm, d), row),
        out_shape=jax.ShapeDtypeStruct((n, d), _F32),
        compiler_params=_params(("parallel",)),
        name="mlp",
    )(x1, o, wco, gm, w1, w2, gf)


def kernel(x_prompt, x_sample, cache_k, cache_v, state_ssm_re, state_ssm_im, cache_mem_k, cache_mem_v,
           page_table, mem_prompt, g_mix, w_in, ssm_a_re, ssm_a_im, ssm_log_dt, ssm_b_re, ssm_b_im,
           ssm_c_re, ssm_c_im, ssm_d, w_glu, b_glu, w_out, g_mem, w_ck, w_cv, g_cross, w_cq, w_co,
           g_mlp, w_ff1, w_ff2, g_final):
    depth = w_in.shape[0]
    assert depth == 1, "single-layer trunk"
    b, t, d = x_prompt.shape
    nseq, dec, _ = x_sample.shape
    a = ATTN_WIDTH
    spg = SEQS_PER_GROUP
    blk = MOBA_BLOCK
    assert t % blk == 0 and b % spg == 0 and nseq % spg == 0 and dec == spg
    n_phys, page = cache_k.shape[1], cache_k.shape[2]
    n_pages = page_table.shape[1]
    past = n_pages * page
    assert past % blk == 0 and blk % page == 0 and dec <= page
    nb = t // blk
    nbp = -(-nb // 8) * 8
    groups, ns_p = ssm_a_re.shape[1], ssm_a_re.shape[2]
    ns = groups * ns_p
    sw = groups * SSM_GROUP
    mtok = mem_prompt.shape[1]
    row2 = lambda z: z.reshape(1, -1)
    hd = (ATTN_HEADS, HEAD_DIM)

    w_in_b = w_in[0].astype(_BF)
    wo_a = w_out[0, :a].astype(_BF)
    wo_s = w_out[0, a:].astype(_BF)
    w_cq_b, w_co_b = w_cq[0].astype(_BF), w_co[0].astype(_BF)
    w1_b, w2_b = w_ff1[0].astype(_BF), w_ff2[0].astype(_BF)
    w_glu_b = w_glu[0].astype(_BF)
    w_ckv_b = jnp.concatenate([w_ck[0], w_cv[0]], axis=1).astype(_BF)
    lbr, lbi, bbr, bbi = _s5_disc(ssm_a_re[0], ssm_a_im[0], ssm_log_dt[0], ssm_b_re[0], ssm_b_im[0])
    lbr, lbi = row2(lbr), row2(lbi)
    bb_r = _block_diag(bbr.reshape(groups, SSM_GROUP, ns_p)).astype(_BF)
    bb_i = _block_diag(bbi.reshape(groups, SSM_GROUP, ns_p)).astype(_BF)
    c_r = _block_diag(jnp.transpose(ssm_c_re[0], (0, 2, 1))).astype(_BF)
    c_i = _block_diag(jnp.transpose(ssm_c_im[0], (0, 2, 1))).astype(_BF)
    s5_w = (lbr, lbi, bb_r, bb_i, c_r, c_i, row2(ssm_d[0]), w_glu_b, row2(b_glu[0]))

    tm = 512 if t % 512 == 0 else blk
    q_p, kt_p, vt_p, kb_p, vtb_p, u_p, km_p = _in_proj_prompt(x_prompt, row2(g_mix[0]), w_in_b, tm=tm)
    km = km_p.reshape(b, nb, ATTN_HEADS, HEAD_DIM)
    km = jnp.pad(jnp.transpose(km, (0, 2, 1, 3)), ((0, 0), (0, 0), (0, nbp - nb), (0, 0)))
    kmblk = (km[:, :, :, None, :] * jnp.eye(ATTN_HEADS, dtype=_F32)[None, :, None, :, None]
             ).reshape(b, ATTN_HEADS * nbp, a)
    attn_p = _moba_prompt(q_p, kb_p.reshape(b, t, a), vtb_p, kmblk)

    zeros_state = jnp.zeros((b // spg, spg, ns), _F32)
    tc = 64 if t % 64 == 0 else t
    ssm_p, hr_p, hi_p = _s5(u_p, zeros_state, zeros_state, *s5_w, tc=tc, seq_lanes=True)

    nt = t // tm
    ssm_map_p = lambda bi, ti: (bi // spg * nt + ti, bi % spg)
    mk_p, mv_p, mkb_p, mvb_p = _mem_kv(mem_prompt.reshape(b * mtok, d), row2(g_mem[0]), w_ckv_b, tm=mtok)
    mem4 = (b, mtok, MEM_HEADS, MEM_HEAD_DIM)
    mw = MEM_HEADS * MEM_HEAD_DIM
    y_p = _post(x_prompt.reshape(b * t, d), attn_p, ssm_p.reshape(b // spg * t, spg * sw), ssm_map_p,
                mkb_p.reshape(b, mtok, mw), mvb_p.reshape(b, mtok, mw),
                wo_a, wo_s, row2(g_cross[0]), w_cq_b, w_co_b, row2(g_mlp[0]), w1_b, w2_b, row2(g_final),
                nb=b, nt=nt, tm=tm)

    n_s = nseq * dec
    tms = 256 if n_s % 256 == 0 else n_s
    q_s, k_s, v_s, u_s = _norm_proj(x_sample.reshape(n_s, d), row2(g_mix[0]), w_in_b, (a, a, a, sw),
                                    tm=tms, name="in_proj_sample")
    nq = ATTN_HEADS * dec
    head_of_row = jnp.arange(nq) // dec
    head_of_col = jnp.arange(a) // HEAD_DIM
    head_mask = (head_of_row[:, None] == head_of_col[None, :]).astype(_F32)
    wq = jnp.tile(q_s.reshape(nseq, dec, a), (1, ATTN_HEADS, 1)) * head_mask[None]
    pad_new = lambda z: jnp.pad(z.reshape(nseq, dec, a), ((0, 0), (0, page - dec), (0, 0))).astype(_BF)
    slope_rows = jnp.asarray(_slopes(), _F32)[head_of_row].reshape(nq, 1)
    qpos_rows = (past + jnp.arange(nq) % dec).astype(_F32).reshape(nq, 1)
    token_minor = lambda c: jnp.transpose(c[0], (0, 2, 3, 1)).reshape(n_phys, a, page)
    pg = 32 if n_pages % 32 == 0 else blk // page
    attn_s = _moba_sample(page_table, wq, pad_new(k_s), pad_new(v_s), slope_rows, qpos_rows,
                          token_minor(cache_k), token_minor(cache_v), pg=pg, past=past)

    to_groups = lambda z: jnp.transpose(z.reshape(nseq // spg, spg, dec, -1), (0, 2, 1, 3))
    u_g = to_groups(u_s).reshape(nseq // spg, dec * spg, sw)
    h0r = state_ssm_re[0].reshape(nseq // spg, spg, ns)
    h0i = state_ssm_im[0].reshape(nseq // spg, spg, ns)
    ssm_g, hr_s, hi_s = _s5(u_g, h0r, h0i, *s5_w, tc=dec, seq_lanes=False)
    ssm_s = jnp.transpose(ssm_g.reshape(nseq // spg, dec, spg, sw), (0, 2, 1, 3)).reshape(n_s, sw)

    nts = n_s // tms
    x1_s, cq_s = _mix(x_sample.reshape(n_s, d), attn_s.reshape(n_s, a), ssm_s, lambda bi, ti: (ti, 0),
                      wo_a, wo_s, row2(g_cross[0]), w_cq_b, nb=1, nt=nts, tm=tms)
    mem_rows = lambda c: c[0].reshape(nseq, mtok * MEM_HEADS, MEM_HEAD_DIM)
    o_s = _cross_attn(cq_s, mem_rows(cache_mem_k), mem_rows(cache_mem_v), sb=spg, rq=dec)
    y_s = _mlp(x1_s, o_s, w_co_b, row2(g_mlp[0]), w1_b, w2_b, row2(g_final), tm=tms)

    kv_out = lambda z: jnp.transpose(z.reshape((1, b) + hd + (t,)), (0, 1, 4, 2, 3))
    return (y_p.reshape(b, t, d), y_s.reshape(nseq, dec, d),
            kv_out(kt_p), kv_out(vt_p),
            hr_p.reshape(1, b, groups, ns_p), hi_p.reshape(1, b, groups, ns_p),
            mk_p.reshape((1,) + mem4), mv_p.reshape((1,) + mem4),
            k_s.reshape((1, nseq, dec) + hd), v_s.reshape((1, nseq, dec) + hd),
            hr_s.reshape(1, nseq, groups, ns_p), hi_s.reshape(1, nseq, groups, ns_p))
```

```python
import functools

import jax
import jax.numpy as jnp
from jax import lax
from jax.experimental import pallas as pl
from jax.experimental.pallas import tpu as pltpu

ATTN_HEADS = 8
HEAD_DIM = 64
ATTN_WIDTH = ATTN_HEADS * HEAD_DIM
MOBA_BLOCK = 256
MOBA_TOPK = 3
SSM_GROUP = 16
MEM_HEADS = 4
MEM_HEAD_DIM = 128
RMS_EPS = 1e-6
SEQS_PER_GROUP = 8
LANES = 128
MXU_N = 256
VMEM_LIMIT = 56 * 1024 * 1024

_BF = jnp.bfloat16
_F32 = jnp.float32
_NEG = -1e30
_BIG = 1e30
_LOG2E = 1.4426950408889634
_NT = (((1,), (1,)), ((), ()))


def _slopes():
    return [2.0 ** (-8.0 * (h + 1) / ATTN_HEADS) for h in range(ATTN_HEADS)]


def _rms(x, g):
    return x * lax.rsqrt(jnp.mean(x * x, axis=-1, keepdims=True) + RMS_EPS) * g


def _dot(a, b):
    return jnp.dot(a, b, preferred_element_type=_F32)


def _dot_nt(a, b):
    return lax.dot_general(a, b, _NT, preferred_element_type=_F32)


def _split_bf16(x):
    hi = x.astype(_BF)
    lo = (x - hi.astype(_F32)).astype(_BF)
    return hi, lo


def _dot_nt_3pass(a, b):
    a_hi, a_lo = _split_bf16(a)
    b_hi, b_lo = _split_bf16(b)
    return _dot_nt(a_hi, b_hi) + _dot_nt(a_hi, b_lo) + _dot_nt(a_lo, b_hi)


def _rank_update(rank, gm, g, tie):
    return rank + jnp.where(gm > g, 1.0, jnp.where(gm == g, tie, 0.0))


def _params(sem):
    return pltpu.CompilerParams(dimension_semantics=sem, vmem_limit_bytes=VMEM_LIMIT)


def _in_proj_prompt_kernel(x_ref, g_ref, w_ref, q_ref, kt_ref, vt_ref, kb_ref, vtb_ref, u_ref, km_ref):
    a = ATTN_WIDTH
    hn = _rms(x_ref[...], g_ref[...]).astype(_BF)
    q_ref[...] = _dot(hn, w_ref[:, 0:a])
    k = _dot(hn, w_ref[:, a:2 * a])
    v = _dot(hn, w_ref[:, 2 * a:3 * a])
    kt_ref[...] = k.T
    vt = v.T
    vt_ref[...] = vt
    kb_ref[...] = k.astype(_BF)
    u_ref[...] = _dot(hn, w_ref[:, 3 * a:])
    for r in range(k.shape[0] // MOBA_BLOCK):
        rows = slice(r * MOBA_BLOCK, (r + 1) * MOBA_BLOCK)
        vtb_ref[r] = vt[:, rows].astype(_BF)
        km_ref[r] = jnp.mean(k[rows, :], axis=0, keepdims=True)


def _in_proj_prompt(x, g, w, *, tm):
    b, t, d = x.shape
    a = ATTN_WIDTH
    su = w.shape[1] - 3 * a
    nt = t // tm
    nbt = tm // MOBA_BLOCK
    nbg = b // SEQS_PER_GROUP
    row = lambda bi, ti: (bi * nt + ti, 0)
    tok_minor = lambda bi, ti: (bi, 0, ti)
    return pl.pallas_call(
        _in_proj_prompt_kernel,
        grid=(b, nt),
        in_specs=[pl.BlockSpec((tm, d), row),
                  pl.BlockSpec((1, d), lambda bi, ti: (0, 0)),
                  pl.BlockSpec(w.shape, lambda bi, ti: (0, 0))],
        out_specs=[pl.BlockSpec((tm, a), row),
                   pl.BlockSpec((None, a, tm), tok_minor),
                   pl.BlockSpec((None, a, tm), tok_minor),
                   pl.BlockSpec((tm, a), row),
                   pl.BlockSpec((None, nbt, a, MOBA_BLOCK), lambda bi, ti: (bi, ti, 0, 0)),
                   pl.BlockSpec((None, tm, su), lambda bi, ti: (bi // SEQS_PER_GROUP, ti, bi % SEQS_PER_GROUP)),
                   pl.BlockSpec((nbt, 1, a), lambda bi, ti: (bi * nt + ti, 0, 0))],
        out_shape=[jax.ShapeDtypeStruct((b * t, a), _F32),
                   jax.ShapeDtypeStruct((b, a, t), _F32),
                   jax.ShapeDtypeStruct((b, a, t), _F32),
                   jax.ShapeDtypeStruct((b * t, a), _BF),
                   jax.ShapeDtypeStruct((b, t // MOBA_BLOCK, a, MOBA_BLOCK), _BF),
                   jax.ShapeDtypeStruct((nbg, t, SEQS_PER_GROUP * su), _F32),
                   jax.ShapeDtypeStruct((b * t // MOBA_BLOCK, 1, a), _F32)],
        compiler_params=_params(("parallel", "parallel")),
        name="in_proj_prompt",
    )(x.reshape(b * t, d), g, w)


def _norm_proj_kernel(x_ref, g_ref, w_ref, *o_refs):
    hn = _rms(x_ref[...], g_ref[...]).astype(_BF)
    c0 = 0
    for o_ref in o_refs:
        c1 = c0 + o_ref.shape[1]
        o_ref[...] = _dot(hn, w_ref[:, c0:c1])
        c0 = c1


def _norm_proj(x, g, w, widths, *, tm, name):
    n, d = x.shape
    row = lambda i: (i, 0)
    return pl.pallas_call(
        _norm_proj_kernel,
        grid=(n // tm,),
        in_specs=[pl.BlockSpec((tm, d), row),
                  pl.BlockSpec((1, d), lambda i: (0, 0)),
                  pl.BlockSpec(w.shape, lambda i: (0, 0))],
        out_specs=[pl.BlockSpec((tm, c), row) for c in widths],
        out_shape=[jax.ShapeDtypeStruct((n, c), _F32) for c in widths],
        compiler_params=_params(("parallel",)),
        name=name,
    )(x, g, w)


def _moba_prompt_kernel(q_ref, k_ref, vt_ref, km_ref, o_ref,
                        sel_ref, bias_ref, qx_ref, sa_ref, sb_ref, m_ref, l_ref, acc_ref, *, nb, nbp):
    blk = MOBA_BLOCK
    nh = ATTN_HEADS
    i = pl.program_id(1)
    q = q_ref[...]
    slopes2 = [s * _LOG2E for s in _slopes()]
    kk = lax.broadcasted_iota(jnp.int32, (blk, blk), 0)
    qq = lax.broadcasted_iota(jnp.int32, (blk, blk), 1)
    dkq = (kk - qq).astype(_F32)

    @pl.when(i == 0)
    def _fill_bias():
        for h in range(nh):
            bias_ref[h] = slopes2[h] * dkq

    gt = _dot_nt_3pass(km_ref[...], q)
    rows = lax.broadcasted_iota(jnp.int32, (nbp, blk), 0)
    fully_past = rows < i
    g = [jnp.where(fully_past, gt[h * nbp:(h + 1) * nbp, :], _NEG) for h in range(nh)]
    rank = [jnp.zeros((nbp, blk), _F32) for _ in range(nh)]
    for m in range(nb):
        tie = jnp.where(rows > m, 1.0, 0.0)
        for h in range(nh):
            rank[h] = _rank_update(rank[h], g[h][m:m + 1, :], g[h], tie)
    for h in range(nh):
        sel_ref[h * nbp:(h + 1) * nbp, :] = jnp.where(fully_past & (rank[h] < MOBA_TOPK), 1.0, 0.0)

    qs = (q * (HEAD_DIM ** -0.5 * _LOG2E)).astype(_BF)
    lane = lax.broadcasted_iota(jnp.int32, (blk, LANES), 1)
    zero = jnp.zeros((), _BF)
    for p in range(nh // 2):
        qp = qs[:, p * LANES:(p + 1) * LANES]
        qx_ref[p, 0:blk, :] = jnp.where(lane < HEAD_DIM, qp, zero)
        qx_ref[p, blk:2 * blk, :] = jnp.where(lane >= HEAD_DIM, qp, zero)

    causal_neg = jnp.where(dkq <= 0.0, 0.0, _NEG)
    ones_rows = jnp.ones((8, blk), _BF)

    def scores(j, s_out, p):
        start = pl.multiple_of(j * blk, blk)
        kb = k_ref[pl.ds(start, blk), p * LANES:(p + 1) * LANES]
        s_out[p] = _dot_nt(kb, qx_ref[p])

    def step(j_next, s_next, j, s_cur, own):
        coff = None if own else ((i - j) * blk).astype(_F32)
        for p in range(nh // 2):
            if j_next is not None:
                scores(j_next, s_next, p)
            vb = vt_ref[j, p * LANES:(p + 1) * LANES, :]
            ptbs = []
            alphas = []
            for x in range(2):
                h = 2 * p + x
                s = s_cur[p, :, x * blk:(x + 1) * blk] + bias_ref[h]
                if own:
                    s = s + causal_neg
                    m_new = jnp.max(s, axis=0, keepdims=True)
                    t = m_new
                    alphas.append(None)
                else:
                    c = slopes2[h] * coff
                    picked = sel_ref[pl.ds(h * nbp + j, 1), :] > 0.5
                    m_old = m_ref[h]
                    m_new = jnp.where(picked, jnp.maximum(m_old, jnp.max(s, axis=0, keepdims=True) - c), m_old)
                    t = jnp.where(picked, m_new + c, _BIG)
                    alphas.append(jnp.exp2(m_old - m_new))
                m_ref[h] = m_new
                ptbs.append(jnp.exp2(s - t).astype(_BF))
            for x in range(2):
                h = 2 * p + x
                arow = slice(h * HEAD_DIM, (h + 1) * HEAD_DIM)
                psum = _dot(ones_rows, ptbs[x])[0:1, :]
                pv = _dot(vb[x * HEAD_DIM:(x + 1) * HEAD_DIM, :], ptbs[x])
                if own:
                    l_ref[h] = psum
                    acc_ref[arow, :] = pv
                else:
                    l_ref[h] = alphas[x] * l_ref[h] + psum
                    acc_ref[arow, :] = acc_ref[arow, :] * alphas[x] + pv

    for p in range(nh // 2):
        scores(i, sa_ref, p)
    step(0, sb_ref, i, sa_ref, True)
    last = jnp.maximum(i - 1, 0)

    def pair(mi, carry):
        b0 = 2 * mi
        step(b0 + 1, sa_ref, b0, sb_ref, False)
        step(jnp.minimum(b0 + 2, last), sb_ref, b0 + 1, sa_ref, False)
        return carry

    lax.fori_loop(0, i // 2, pair, 0)

    @pl.when(i % 2 == 1)
    def _odd_tail():
        step(None, None, i - 1, sb_ref, False)

    for h in range(nh):
        arow = slice(h * HEAD_DIM, (h + 1) * HEAD_DIM)
        o_ref[arow, :] = acc_ref[arow, :] * (1.0 / l_ref[h])


def _moba_prompt(q, kb, vt, kmblk):
    b, t, a = kb.shape
    blk = MOBA_BLOCK
    nb = t // blk
    nbp = kmblk.shape[1] // ATTN_HEADS
    return pl.pallas_call(
        functools.partial(_moba_prompt_kernel, nb=nb, nbp=nbp),
        grid=(b, nb),
        in_specs=[pl.BlockSpec((blk, a), lambda bi, i: (bi * nb + i, 0)),
                  pl.BlockSpec((None, t, a), lambda bi, i: (bi, 0, 0)),
                  pl.BlockSpec((None, nb, a, blk), lambda bi, i: (bi, 0, 0, 0)),
                  pl.BlockSpec((None,) + kmblk.shape[1:], lambda bi, i: (bi, 0, 0))],
        out_specs=pl.BlockSpec((None, a, blk), lambda bi, i: (bi, 0, i)),
        out_shape=jax.ShapeDtypeStruct((b, a, t), _F32),
        scratch_shapes=[pltpu.VMEM((ATTN_HEADS * nbp, blk), _F32),
                        pltpu.VMEM((ATTN_HEADS, blk, blk), _F32),
                        pltpu.VMEM((ATTN_HEADS // 2, 2 * blk, LANES), _BF),
                        pltpu.VMEM((ATTN_HEADS // 2, blk, 2 * blk), _F32),
                        pltpu.VMEM((ATTN_HEADS // 2, blk, 2 * blk), _F32),
                        pltpu.VMEM((ATTN_HEADS, 1, blk), _F32),
                        pltpu.VMEM((ATTN_HEADS, 1, blk), _F32),
                        pltpu.VMEM((a, blk), _F32)],
        compiler_params=_params(("parallel", "arbitrary")),
        name="moba_prompt",
    )(q, kb, vt, kmblk)


def _moba_sample_kernel(pt_ref, wq_ref, kn_ref, vn_ref, slope_ref, qpos_ref, *rest,
                        pg, n_pages, page, past, dec):
    del pt_ref
    k_refs = rest[:pg]
    v_refs = rest[pg:2 * pg]
    o_ref, s_ref, gate_ref, acc_ref, l_ref = rest[2 * pg:]
    blk = MOBA_BLOCK
    ppb = blk // page
    nb = n_pages // ppb
    n_groups = n_pages // pg
    step = pl.program_id(1)
    wq = wq_ref[...]
    nq = wq.shape[0]
    wqb = (wq * (HEAD_DIM ** -0.5 * _LOG2E)).astype(_BF)
    blk_lane = lax.broadcasted_iota(jnp.int32, (nq, LANES), 1)

    @pl.when(step == 0)
    def _init():
        gate_ref[...] = jnp.zeros(gate_ref.shape, _F32)

    @pl.when(step < n_groups)
    def _k_phase():
        for c in range(pg // ppb):
            gs = jnp.zeros((nq, page), _F32)
            for r in range(c * ppb, (c + 1) * ppb):
                st = _dot(wqb, k_refs[r][...].astype(_BF))
                s_ref[step * pg + r] = st
                gs = gs + st
            n = step * (pg // ppb) + c
            gsum = jnp.sum(gs, axis=1, keepdims=True)
            gate_ref[...] = jnp.where(blk_lane == n, gsum, gate_ref[...])

    @pl.when(step == n_groups - 1)
    def _softmax():
        gate = gate_ref[...]
        rank = jnp.zeros(gate.shape, _F32)
        for m in range(nb):
            tie = jnp.where(blk_lane > m, 1.0, 0.0)
            rank = _rank_update(rank, gate[:, m:m + 1], gate, tie)
        drop = jnp.where(rank < MOBA_TOPK, 0.0, _NEG)
        lane = lax.broadcasted_iota(jnp.int32, (nq, page), 1).astype(_F32)
        slope2 = jnp.broadcast_to(slope_ref[...] * _LOG2E, (nq, page))
        qpos = jnp.broadcast_to(qpos_ref[...], (nq, page))
        base = slope2 * (lane - qpos)

        mm = jnp.full((nq, page), _NEG, _F32)
        for p in range(n_pages):
            n = p // ppb
            z = s_ref[p] + (base + slope2 * float(p * page)) + drop[:, n:n + 1]
            s_ref[p] = z
            mm = jnp.maximum(mm, z)
        so = _dot_nt(wqb, kn_ref[...]) + (base + slope2 * float(past))
        so = jnp.where(lane + float(past) <= qpos, so, _NEG)
        m = jnp.max(jnp.maximum(mm, so), axis=1, keepdims=True)
        ls = jnp.exp2(so - m)
        acc_ref[...] = _dot(ls.astype(_BF), vn_ref[...])
        for p in range(n_pages):
            pp = jnp.exp2(s_ref[p] - m)
            s_ref[p] = pp
            ls = ls + pp
        l_ref[...] = jnp.sum(ls, axis=1, keepdims=True)

    @pl.when(step >= n_groups)
    def _v_phase():
        acc = acc_ref[...]
        for r in range(pg):
            pp = s_ref[(step - n_groups) * pg + r]
            acc = acc + _dot_nt(pp.astype(_BF), v_refs[r][...].astype(_BF))
        acc_ref[...] = acc

    @pl.when(step == 2 * n_groups - 1)
    def _finish():
        o = acc_ref[...] * (1.0 / l_ref[...])
        lane_a = lax.broadcasted_iota(jnp.int32, (dec, o.shape[1]), 1)
        out = jnp.zeros((dec, o.shape[1]), _F32)
        for h in range(ATTN_HEADS):
            in_head = (lane_a >= h * HEAD_DIM) & (lane_a < (h + 1) * HEAD_DIM)
            out = out + jnp.where(in_head, o[h * dec:(h + 1) * dec, :], 0.0)
        o_ref[...] = out


def _moba_sample(page_table, wq, kn, vn, slope_rows, qpos_rows, cache_kt, cache_vt, *, pg, past):
    nseq, n_pages = page_table.shape
    _, a, page = cache_kt.shape
    nq = wq.shape[1]
    dec = nq // ATTN_HEADS
    n_groups = n_pages // pg
    assert n_pages * page // MOBA_BLOCK <= LANES
    seq = lambda bi, s, pt: (bi, 0, 0)
    const = lambda bi, s, pt: (0, 0)

    def k_map(r):
        return lambda bi, s, pt: (pt[bi, jnp.minimum(s, n_groups - 1) * pg + r], 0, 0)

    def v_map(r):
        def index(bi, s, pt):
            in_v = s >= n_groups
            seq_i = jnp.where(in_v, bi, jnp.maximum(bi - 1, 0))
            grp = jnp.where(in_v, s - n_groups, n_groups - 1)
            return (pt[seq_i, grp * pg + r], 0, 0)
        return index

    in_specs = [pl.BlockSpec((None, nq, a), seq),
                pl.BlockSpec((None, page, a), seq),
                pl.BlockSpec((None, page, a), seq),
                pl.BlockSpec((nq, 1), const),
                pl.BlockSpec((nq, 1), const)]
    in_specs += [pl.BlockSpec((None, a, page), k_map(r)) for r in range(pg)]
    in_specs += [pl.BlockSpec((None, a, page), v_map(r)) for r in range(pg)]
    return pl.pallas_call(
        functools.partial(_moba_sample_kernel, pg=pg, n_pages=n_pages, page=page, past=past, dec=dec),
        grid_spec=pltpu.PrefetchScalarGridSpec(
            num_scalar_prefetch=1,
            grid=(nseq, 2 * n_groups),
            in_specs=in_specs,
            out_specs=pl.BlockSpec((None, dec, a), seq),
            scratch_shapes=[pltpu.VMEM((n_pages, nq, page), _F32),
                            pltpu.VMEM((nq, LANES), _F32),
                            pltpu.VMEM((nq, a), _F32),
                            pltpu.VMEM((nq, 1), _F32)]),
        out_shape=jax.ShapeDtypeStruct((nseq, dec, a), _F32),
        compiler_params=_params(("parallel", "arbitrary")),
        name="moba_sample",
    )(page_table, wq, kn, vn, slope_rows, qpos_rows, *([cache_kt] * pg), *([cache_vt] * pg))


def _s5_disc_kernel(are_ref, aim_ref, ldt_ref, bre_ref, bim_ref, lbr_ref, lbi_ref, bbr_ref, bbi_ref):
    are = are_ref[...]
    aim = aim_ref[...]
    dt = jnp.exp(ldt_ref[...])
    mag = jnp.exp(are * dt)
    ang = aim * dt
    lb_re = mag * jnp.cos(ang)
    lb_im = mag * jnp.sin(ang)
    e_re = lb_re - 1.0
    e_im = lb_im
    den = are * are + aim * aim
    f_re = (e_re * are + e_im * aim) / den
    f_im = (e_im * are - e_re * aim) / den
    br = bre_ref[...]
    bi = bim_ref[...]
    lbr_ref[...] = lb_re
    lbi_ref[...] = lb_im
    bbr_ref[...] = f_re * br - f_im * bi
    bbi_ref[...] = f_re * bi + f_im * br


def _s5_disc(a_re, a_im, log_dt, b_re, b_im):
    g, p, c = b_re.shape
    rep = lambda z: jnp.repeat(z, c, axis=0)
    are, aim = rep(a_re), rep(a_im)
    ldt = rep(log_dt.reshape(g, 1))
    b2 = lambda z: jnp.transpose(z, (0, 2, 1)).reshape(g * c, p)
    shp = jax.ShapeDtypeStruct((g * c, p), _F32)
    lbr, lbi, bbr, bbi = pl.pallas_call(
        _s5_disc_kernel, out_shape=[shp] * 4, name="s5_disc",
    )(are, aim, ldt, b2(b_re), b2(b_im))
    return lbr[::c], lbi[::c], bbr, bbi


def _s5_kernel(u_ref, pm_ref, pmt_ref, h0r_ref, h0i_ref, lbr_ref, lbi_ref, bbr_ref, bbi_ref, cr_ref, ci_ref, d_ref,
               wg_ref, bg_ref, y_ref, hr_out, hi_out, hre, him, st_r, st_i, *, tc, cw, seq_lanes):
    spg = SEQS_PER_GROUP
    t = pl.program_id(1)
    ns = hre.shape[1]
    w = wg_ref.shape[0]
    if seq_lanes:
        u_st = jnp.concatenate([u_ref[:, b * w:(b + 1) * w] for b in range(spg)], axis=0)
        u_hi, u_lo = _split_bf16(u_st)
        u = _dot(pm_ref[...], u_hi) + _dot(pm_ref[...], u_lo)
    else:
        u = u_ref[...]
    ub = u.astype(_BF)
    ratio = ns // w
    for n in range(ns // MXU_N):
        out = slice(n * MXU_N, (n + 1) * MXU_N)
        lo = (n * MXU_N // ratio) // LANES * LANES
        src = slice(lo, lo + LANES)
        hre[:, out] = _dot(ub[:, src], bbr_ref[src, out])
        him[:, out] = _dot(ub[:, src], bbi_ref[src, out])

    @pl.when(t == 0)
    def _init():
        st_r[...] = h0r_ref[...]
        st_i[...] = h0i_ref[...]

    for c in range(ns // cw):
        cs = slice(c * cw, (c + 1) * cw)
        lr = jnp.broadcast_to(lbr_ref[:, cs], (spg, cw))
        li = jnp.broadcast_to(lbi_ref[:, cs], (spg, cw))

        def step(tt, carry):
            sr, si = carry
            r0 = pl.multiple_of(tt * spg, spg)
            nr = lr * sr - li * si + hre[pl.ds(r0, spg), cs]
            ni = lr * si + li * sr + him[pl.ds(r0, spg), cs]
            hre[pl.ds(r0, spg), cs] = nr
            him[pl.ds(r0, spg), cs] = ni
            return nr, ni

        sr, si = lax.fori_loop(0, tc, step, (st_r[:, cs], st_i[:, cs]), unroll=min(tc, 8))
        st_r[:, cs] = sr
        st_i[:, cs] = si

    parts = []
    for m in range(w // MXU_N):
        out = slice(m * MXU_N, (m + 1) * MXU_N)
        src = slice(m * MXU_N * ratio, (m + 1) * MXU_N * ratio)
        parts.append(_dot(hre[:, src].astype(_BF), cr_ref[src, out])
                     - _dot(him[:, src].astype(_BF), ci_ref[src, out]))
    y = jnp.concatenate(parts, axis=1) + d_ref[...] * u
    z = _dot(y.astype(_BF), wg_ref[...]) + bg_ref[...]
    y = y * (1.0 / (1.0 + jnp.exp(-z)))
    if seq_lanes:
        y_st = _dot(pmt_ref[...], y.astype(_BF)).astype(y_ref.dtype)
        for b in range(spg):
            y_ref[:, b * w:(b + 1) * w] = y_st[b * tc:(b + 1) * tc, :]
    else:
        y_ref[...] = y.astype(y_ref.dtype)
    hr_out[...] = st_r[...]
    hi_out[...] = st_i[...]


def _s5(u, h0r, h0i, lbr, lbi, bbr, bbi, cr, ci, d, wg, bg, *, tc, seq_lanes):
    spg = SEQS_PER_GROUP
    w = wg.shape[0]
    nbg = u.shape[0]
    t = u.shape[1] if seq_lanes else u.shape[1] // spg
    ns = lbr.shape[1]
    ratio = ns // w
    assert ns % MXU_N == 0 and w % MXU_N == 0 and LANES % (MXU_N // ratio) == 0
    const = lambda gi, ti: (0, 0)
    grp = lambda gi, ti: (gi, 0, 0)
    io_block = (None, tc, spg * w) if seq_lanes else (None, tc * spg, w)
    r = tc * spg
    dst = jnp.arange(r)
    pm = (dst[:, None] % spg * tc + dst[:, None] // spg == dst[None, :]).astype(_BF)
    return pl.pallas_call(
        functools.partial(_s5_kernel, tc=tc, cw=min(ns, 8 * LANES), seq_lanes=seq_lanes),
        grid=(nbg, t // tc),
        in_specs=[pl.BlockSpec(io_block, lambda gi, ti: (gi, ti, 0)),
                  pl.BlockSpec((r, r), const), pl.BlockSpec((r, r), const),
                  pl.BlockSpec((None, spg, ns), grp),
                  pl.BlockSpec((None, spg, ns), grp),
                  pl.BlockSpec((1, ns), const), pl.BlockSpec((1, ns), const),
                  pl.BlockSpec((w, ns), const), pl.BlockSpec((w, ns), const),
                  pl.BlockSpec((ns, w), const), pl.BlockSpec((ns, w), const),
                  pl.BlockSpec((1, w), const),
                  pl.BlockSpec((w, w), const), pl.BlockSpec((1, w), const)],
        out_specs=[pl.BlockSpec(io_block, lambda gi, ti: (gi, ti, 0)),
                   pl.BlockSpec((None, spg, ns), grp),
                   pl.BlockSpec((None, spg, ns), grp)],
        out_shape=[jax.ShapeDtypeStruct(u.shape, _BF if seq_lanes else _F32),
                   jax.ShapeDtypeStruct((nbg, spg, ns), _F32),
                   jax.ShapeDtypeStruct((nbg, spg, ns), _F32)],
        scratch_shapes=[pltpu.VMEM((r, ns), _F32), pltpu.VMEM((r, ns), _F32),
                        pltpu.VMEM((spg, ns), _F32), pltpu.VMEM((spg, ns), _F32)],
        compiler_params=_params(("parallel", "arbitrary")),
        name="s5",
    )(u, pm, pm.T, h0r, h0i, lbr, lbi, bbr, bbi, cr, ci, d, wg, bg)


def _block_diag(blocks):
    g, r, c = blocks.shape
    eye = jnp.eye(g, dtype=blocks.dtype)
    return (blocks[:, :, None, :] * eye[:, None, :, None]).reshape(g * r, g * c)


def _mix_kernel(x_ref, a_ref, s_ref, woa_ref, wos_ref, gc_ref, wcq_ref, x1_ref, cq_ref):
    x1 = (x_ref[...] + _dot(a_ref[...].astype(_BF), woa_ref[...])
          + _dot(s_ref[...].astype(_BF), wos_ref[...]))
    x1_ref[...] = x1
    cq_ref[...] = _dot(_rms(x1, gc_ref[...]).astype(_BF), wcq_ref[...])


def _mix(x, attn, ssm, ssm_map, woa, wos, gc, wcq, *, nb, nt, tm):
    n, d = x.shape
    row = lambda bi, ti: (bi * nt + ti, 0)
    const = lambda bi, ti: (0, 0)
    mw = wcq.shape[1]
    sw = wos.shape[0]
    return pl.pallas_call(
        _mix_kernel,
        grid=(nb, nt),
        in_specs=[pl.BlockSpec((tm, d), row),
                  pl.BlockSpec((tm, attn.shape[1]), row),
                  pl.BlockSpec((tm, sw), ssm_map),
                  pl.BlockSpec(woa.shape, const), pl.BlockSpec(wos.shape, const),
                  pl.BlockSpec((1, d), const), pl.BlockSpec(wcq.shape, const)],
        out_specs=[pl.BlockSpec((tm, d), row), pl.BlockSpec((tm, mw), row)],
        out_shape=[jax.ShapeDtypeStruct((n, d), _F32), jax.ShapeDtypeStruct((n, mw), _F32)],
        compiler_params=_params(("parallel", "parallel")),
        name="mix",
    )(x, attn, ssm, woa, wos, gc, wcq)


def _mem_kv_kernel(x_ref, g_ref, w_ref, mk_ref, mv_ref, mkb_ref, mvb_ref):
    hn = _rms(x_ref[...], g_ref[...]).astype(_BF)
    mw = MEM_HEADS * MEM_HEAD_DIM
    k = _dot(hn, w_ref[:, :mw])
    v = _dot(hn, w_ref[:, mw:])
    mkb_ref[...] = k.astype(_BF)
    mvb_ref[...] = v.astype(_BF)
    for h in range(MEM_HEADS):
        cols = slice(h * MEM_HEAD_DIM, (h + 1) * MEM_HEAD_DIM)
        mk_ref[:, h, :] = k[:, cols]
        mv_ref[:, h, :] = v[:, cols]


def _mem_kv(mem, g, w, *, tm):
    n, d = mem.shape
    mw = MEM_HEADS * MEM_HEAD_DIM
    shp = jax.ShapeDtypeStruct((n, MEM_HEADS, MEM_HEAD_DIM), _F32)
    spec = pl.BlockSpec((tm, MEM_HEADS, MEM_HEAD_DIM), lambda i: (i, 0, 0))
    flat = pl.BlockSpec((tm, mw), lambda i: (i, 0))
    return pl.pallas_call(
        _mem_kv_kernel,
        grid=(n // tm,),
        in_specs=[pl.BlockSpec((tm, d), lambda i: (i, 0)),
                  pl.BlockSpec((1, d), lambda i: (0, 0)),
                  pl.BlockSpec(w.shape, lambda i: (0, 0))],
        out_specs=[spec, spec, flat, flat],
        out_shape=[shp, shp, jax.ShapeDtypeStruct((n, mw), _BF), jax.ShapeDtypeStruct((n, mw), _BF)],
        compiler_params=_params(("parallel",)),
        name="mem_kv",
    )(mem, g, w)


def _cross_attn_kernel(q_ref, mk_ref, mv_ref, o_ref, *, sb, rq):
    nrow = MEM_HEADS * rq
    nmem = mk_ref.shape[1]
    row_head = lax.broadcasted_iota(jnp.int32, (nrow, nmem), 0) // rq
    col_head = lax.broadcasted_iota(jnp.int32, (nrow, nmem), 1) % MEM_HEADS
    same_head = row_head == col_head
    for s in range(sb):
        q = q_ref[s * rq:(s + 1) * rq, :]
        wq = jnp.concatenate([q[:, h * MEM_HEAD_DIM:(h + 1) * MEM_HEAD_DIM] for h in range(MEM_HEADS)], axis=0)
        sc = _dot_nt(wq.astype(_BF), mk_ref[s].astype(_BF)) * (MEM_HEAD_DIM ** -0.5)
        sc = jnp.where(same_head, sc, _NEG)
        m = jnp.max(sc, axis=-1, keepdims=True)
        e = jnp.exp(sc - m)
        p = e * (1.0 / jnp.sum(e, axis=-1, keepdims=True))
        o = _dot(p.astype(_BF), mv_ref[s].astype(_BF))
        o_ref[s * rq:(s + 1) * rq, :] = jnp.concatenate(
            [o[h * rq:(h + 1) * rq, :] for h in range(MEM_HEADS)], axis=1)


def _cross_attn(cq, mk, mv, *, sb, rq):
    n, mw = cq.shape
    nseq, nmem, dh = mk.shape
    mem = lambda si: (si, 0, 0)
    return pl.pallas_call(
        functools.partial(_cross_attn_kernel, sb=sb, rq=rq),
        grid=(nseq // sb,),
        in_specs=[pl.BlockSpec((sb * rq, mw), lambda si: (si, 0)),
                  pl.BlockSpec((sb, nmem, dh), mem),
                  pl.BlockSpec((sb, nmem, dh), mem)],
        out_specs=pl.BlockSpec((sb * rq, mw), lambda si: (si, 0)),
        out_shape=jax.ShapeDtypeStruct((n, mw), _F32),
        compiler_params=_params(("parallel",)),
        name="cross_attn",
    )(cq, mk, mv)


def _mlp_tail(x2, gm_ref, w1_ref, w2_ref, gf_ref, fc):
    hn = _rms(x2, gm_ref[...]).astype(_BF)
    acc = x2
    for c in range(w1_ref.shape[1] // fc):
        cols = slice(c * fc, (c + 1) * fc)
        hid = jnp.square(jnp.maximum(_dot(hn, w1_ref[:, cols]), 0.0)).astype(_BF)
        acc = acc + _dot(hid, w2_ref[cols, :])
    return _rms(acc, gf_ref[...])


def _mlp_kernel(x1_ref, o_ref, wco_ref, gm_ref, w1_ref, w2_ref, gf_ref, y_ref, *, fc):
    x2 = x1_ref[...] + _dot(o_ref[...].astype(_BF), wco_ref[...])
    y_ref[...] = _mlp_tail(x2, gm_ref, w1_ref, w2_ref, gf_ref, fc)


def _post_kernel(x_ref, a_ref, s_ref, mk_ref, mv_ref, woa_ref, wos_ref, gc_ref, wcq_ref, wco_ref,
                 gm_ref, w1_ref, w2_ref, gf_ref, y_ref, *, fc):
    x1 = (x_ref[...] + _dot(a_ref[...].T.astype(_BF), woa_ref[...])
          + _dot(s_ref[...].astype(_BF), wos_ref[...]))
    cq = _dot(_rms(x1, gc_ref[...]).astype(_BF), wcq_ref[...]).astype(_BF)
    heads = []
    for h in range(MEM_HEADS):
        cols = slice(h * MEM_HEAD_DIM, (h + 1) * MEM_HEAD_DIM)
        sc = _dot_nt(cq[:, cols], mk_ref[:, cols]) * (MEM_HEAD_DIM ** -0.5)
        m = jnp.max(sc, axis=-1, keepdims=True)
        e = jnp.exp(sc - m)
        p = e * (1.0 / jnp.sum(e, axis=-1, keepdims=True))
        heads.append(_dot(p.astype(_BF), mv_ref[:, cols]))
    o = jnp.concatenate(heads, axis=1)
    x2 = x1 + _dot(o.astype(_BF), wco_ref[...])
    y_ref[...] = _mlp_tail(x2, gm_ref, w1_ref, w2_ref, gf_ref, fc)


def _post(x, attn, ssm, ssm_map, mkb, mvb, woa, wos, gc, wcq, wco, gm, w1, w2, gf, *, nb, nt, tm):
    n, d = x.shape
    row = lambda bi, ti: (bi * nt + ti, 0)
    const = lambda bi, ti: (0, 0)
    mem = lambda bi, ti: (bi, 0, 0)
    resident = lambda z: pl.BlockSpec(z.shape, const, pipeline_mode=pl.Buffered(1))
    return pl.pallas_call(
        functools.partial(_post_kernel, fc=min(w1.shape[1], 1024)),
        grid=(nb, nt),
        in_specs=[pl.BlockSpec((tm, d), row),
                  pl.BlockSpec((None, attn.shape[1], tm), lambda bi, ti: (bi, 0, ti)),
                  pl.BlockSpec((tm, wos.shape[0]), ssm_map),
                  pl.BlockSpec((None,) + mkb.shape[1:], mem), pl.BlockSpec((None,) + mvb.shape[1:], mem),
                  resident(woa), resident(wos), resident(gc), resident(wcq), resident(wco),
                  resident(gm), resident(w1), resident(w2), resident(gf)],
        out_specs=pl.BlockSpec((tm, d), row),
        out_shape=jax.ShapeDtypeStruct((n, d), _F32),
        compiler_params=_params(("parallel", "parallel")),
        name="post",
    )(x, attn, ssm, mkb, mvb, woa, wos, gc, wcq, wco, gm, w1, w2, gf)


def _mlp(x1, o, wco, gm, w1, w2, gf, *, tm):
    n, d = x1.shape
    row = lambda i: (i, 0)
    const = lambda i: (0, 0)
    return pl.pallas_call(
        functools.partial(_mlp_kernel, fc=min(w1.shape[1], 1024)),
        grid=(n // tm,),
        in_specs=[pl.BlockSpec((tm, d), row), pl.BlockSpec((tm, o.shape[1]), row),
                  pl.BlockSpec(wco.shape, const), pl.BlockSpec((1, d), const),
                  pl.BlockSpec(w1.shape, const), pl.BlockSpec(w2.shape, const),
                  pl.BlockSpec((1, d), const)],
        out_specs=pl.BlockSpec((tm, d), row),
        out_shape=jax.ShapeDtypeStruct((n, d), _F32),
        compiler_params=_params(("parallel",)),
        name="mlp",
    )(x1, o, wco, gm, w1, w2, gf)


def kernel(x_prompt, x_sample, cache_k, cache_v, state_ssm_re, state_ssm_im, cache_mem_k, cache_mem_v,
           page_table, mem_prompt, g_mix, w_in, ssm_a_re, ssm_a_im, ssm_log_dt, ssm_b_re, ssm_b_im,
           ssm_c_re, ssm_c_im, ssm_d, w_glu, b_glu, w_out, g_mem, w_ck, w_cv, g_cross, w_cq, w_co,
           g_mlp, w_ff1, w_ff2, g_final):
    depth = w_in.shape[0]
    assert depth == 1, "single-layer trunk"
    b, t, d = x_prompt.shape
    nseq, dec, _ = x_sample.shape
    a = ATTN_WIDTH
    spg = SEQS_PER_GROUP
    blk = MOBA_BLOCK
    assert t % blk == 0 and b % spg == 0 and nseq % spg == 0 and dec == spg
    n_phys, page = cache_k.shape[1], cache_k.shape[2]
    n_pages = page_table.shape[1]
    past = n_pages * page
    assert past % blk == 0 and blk % page == 0 and dec <= page
    nb = t // blk
    nbp = -(-nb // 8) * 8
    groups, ns_p = ssm_a_re.shape[1], ssm_a_re.shape[2]
    ns = groups * ns_p
    sw = groups * SSM_GROUP
    mtok = mem_prompt.shape[1]
    row2 = lambda z: z.reshape(1, -1)
    hd = (ATTN_HEADS, HEAD_DIM)

    w_in_b = w_in[0].astype(_BF)
    wo_a = w_out[0, :a].astype(_BF)
    wo_s = w_out[0, a:].astype(_BF)
    w_cq_b, w_co_b = w_cq[0].astype(_BF), w_co[0].astype(_BF)
    w1_b, w2_b = w_ff1[0].astype(_BF), w_ff2[0].astype(_BF)
    w_glu_b = w_glu[0].astype(_BF)
    w_ckv_b = jnp.concatenate([w_ck[0], w_cv[0]], axis=1).astype(_BF)
    lbr, lbi, bbr, bbi = _s5_disc(ssm_a_re[0], ssm_a_im[0], ssm_log_dt[0], ssm_b_re[0], ssm_b_im[0])
    lbr, lbi = row2(lbr), row2(lbi)
    bb_r = _block_diag(bbr.reshape(groups, SSM_GROUP, ns_p)).astype(_BF)
    bb_i = _block_diag(bbi.reshape(groups, SSM_GROUP, ns_p)).astype(_BF)
    c_r = _block_diag(jnp.transpose(ssm_c_re[0], (0, 2, 1))).astype(_BF)
    c_i = _block_diag(jnp.transpose(ssm_c_im[0], (0, 2, 1))).astype(_BF)
    s5_w = (lbr, lbi, bb_r, bb_i, c_r, c_i, row2(ssm_d[0]), w_glu_b, row2(b_glu[0]))

    tm = 512 if t % 512 == 0 else blk
    q_p, kt_p, vt_p, kb_p, vtb_p, u_p, km_p = _in_proj_prompt(x_prompt, row2(g_mix[0]), w_in_b, tm=tm)
    km = km_p.reshape(b, nb, ATTN_HEADS, HEAD_DIM)
    km = jnp.pad(jnp.transpose(km, (0, 2, 1, 3)), ((0, 0), (0, 0), (0, nbp - nb), (0, 0)))
    kmblk = (km[:, :, :, None, :] * jnp.eye(ATTN_HEADS, dtype=_F32)[None, :, None, :, None]
             ).reshape(b, ATTN_HEADS * nbp, a)
    attn_p = _moba_prompt(q_p, kb_p.reshape(b, t, a), vtb_p, kmblk)

    zeros_state = jnp.zeros((b // spg, spg, ns), _F32)
    tc = 64 if t % 64 == 0 else t
    ssm_p, hr_p, hi_p = _s5(u_p, zeros_state, zeros_state, *s5_w, tc=tc, seq_lanes=True)

    nt = t // tm
    ssm_map_p = lambda bi, ti: (bi // spg * nt + ti, bi % spg)
    mk_p, mv_p, mkb_p, mvb_p = _mem_kv(mem_prompt.reshape(b * mtok, d), row2(g_mem[0]), w_ckv_b, tm=mtok)
    mem4 = (b, mtok, MEM_HEADS, MEM_HEAD_DIM)
    mw = MEM_HEADS * MEM_HEAD_DIM
    y_p = _post(x_prompt.reshape(b * t, d), attn_p, ssm_p.reshape(b // spg * t, spg * sw), ssm_map_p,
                mkb_p.reshape(b, mtok, mw), mvb_p.reshape(b, mtok, mw),
                wo_a, wo_s, row2(g_cross[0]), w_cq_b, w_co_b, row2(g_mlp[0]), w1_b, w2_b, row2(g_final),
                nb=b, nt=nt, tm=tm)

    n_s = nseq * dec
    tms = 256 if n_s % 256 == 0 else n_s
    q_s, k_s, v_s, u_s = _norm_proj(x_sample.reshape(n_s, d), row2(g_mix[0]), w_in_b, (a, a, a, sw),
                                    tm=tms, name="in_proj_sample")
    nq = ATTN_HEADS * dec
    head_of_row = jnp.arange(nq) // dec
    head_of_col = jnp.arange(a) // HEAD_DIM
    head_mask = (head_of_row[:, None] == head_of_col[None, :]).astype(_F32)
    wq = jnp.tile(q_s.reshape(nseq, dec, a), (1, ATTN_HEADS, 1)) * head_mask[None]
    pad_new = lambda z: jnp.pad(z.reshape(nseq, dec, a), ((0, 0), (0, page - dec), (0, 0))).astype(_BF)
    slope_rows = jnp.asarray(_slopes(), _F32)[head_of_row].reshape(nq, 1)
    qpos_rows = (past + jnp.arange(nq) % dec).astype(_F32).reshape(nq, 1)
    token_minor = lambda c: jnp.transpose(c[0], (0, 2, 3, 1)).reshape(n_phys, a, page)
    pg = 32 if n_pages % 32 == 0 else blk // page
    attn_s = _moba_sample(page_table, wq, pad_new(k_s), pad_new(v_s), slope_rows, qpos_rows,
                          token_minor(cache_k), token_minor(cache_v), pg=pg, past=past)

    to_groups = lambda z: jnp.transpose(z.reshape(nseq // spg, spg, dec, -1), (0, 2, 1, 3))
    u_g = to_groups(u_s).reshape(nseq // spg, dec * spg, sw)
    h0r = state_ssm_re[0].reshape(nseq // spg, spg, ns)
    h0i = state_ssm_im[0].reshape(nseq // spg, spg, ns)
    ssm_g, hr_s, hi_s = _s5(u_g, h0r, h0i, *s5_w, tc=dec, seq_lanes=False)
    ssm_s = jnp.transpose(ssm_g.reshape(nseq // spg, dec, spg, sw), (0, 2, 1, 3)).reshape(n_s, sw)

    nts = n_s // tms
    x1_s, cq_s = _mix(x_sample.reshape(n_s, d), attn_s.reshape(n_s, a), ssm_s, lambda bi, ti: (ti, 0),
                      wo_a, wo_s, row2(g_cross[0]), w_cq_b, nb=1, nt=nts, tm=tms)
    mem_rows = lambda c: c[0].reshape(nseq, mtok * MEM_HEADS, MEM_HEAD_DIM)
    o_s = _cross_attn(cq_s, mem_rows(cache_mem_k), mem_rows(cache_mem_v), sb=spg, rq=dec)
    y_s = _mlp(x1_s, o_s, w_co_b, row2(g_mlp[0]), w1_b, w2_b, row2(g_final), tm=tms)

    kv_out = lambda z: jnp.transpose(z.reshape((1, b) + hd + (t,)), (0, 1, 4, 2, 3))
    return (y_p.reshape(b, t, d), y_s.reshape(nseq, dec, d),
            kv_out(kt_p), kv_out(vt_p),
            hr_p.reshape(1, b, groups, ns_p), hi_p.reshape(1, b, groups, ns_p),
            mk_p.reshape((1,) + mem4), mv_p.reshape((1,) + mem4),
            k_s.reshape((1, nseq, dec) + hd), v_s.reshape((1, nseq, dec) + hd),
            hr_s.reshape(1, nseq, groups, ns_p), hi_s.reshape(1, nseq, groups, ns_p))
```

```python
import functools

import jax
import jax.numpy as jnp
from jax import lax
from jax.experimental import pallas as pl
from jax.experimental.pallas import tpu as pltpu

ATTN_HEADS = 8
HEAD_DIM = 64
ATTN_WIDTH = ATTN_HEADS * HEAD_DIM
MOBA_BLOCK = 256
MOBA_TOPK = 3
SSM_GROUP = 16
MEM_HEADS = 4
MEM_HEAD_DIM = 128
RMS_EPS = 1e-6
SEQS_PER_GROUP = 8
LANES = 128
MXU_N = 256
VMEM_LIMIT = 56 * 1024 * 1024

_BF = jnp.bfloat16
_F32 = jnp.float32
_NEG = -1e30
_BIG = 1e30
_LOG2E = 1.4426950408889634
_NT = (((1,), (1,)), ((), ()))


def _slopes():
    return [2.0 ** (-8.0 * (h + 1) / ATTN_HEADS) for h in range(ATTN_HEADS)]


def _rms(x, g):
    return x * lax.rsqrt(jnp.mean(x * x, axis=-1, keepdims=True) + RMS_EPS) * g


def _dot(a, b):
    return jnp.dot(a, b, preferred_element_type=_F32)


def _dot_nt(a, b):
    return lax.dot_general(a, b, _NT, preferred_element_type=_F32)


def _split_bf16(x):
    hi = x.astype(_BF)
    lo = (x - hi.astype(_F32)).astype(_BF)
    return hi, lo


def _dot_nt_3pass(a, b):
    a_hi, a_lo = _split_bf16(a)
    b_hi, b_lo = _split_bf16(b)
    return _dot_nt(a_hi, b_hi) + _dot_nt(a_hi, b_lo) + _dot_nt(a_lo, b_hi)


def _rank_update(rank, gm, g, tie):
    return rank + jnp.where(gm > g, 1.0, jnp.where(gm == g, tie, 0.0))


def _params(sem):
    return pltpu.CompilerParams(dimension_semantics=sem, vmem_limit_bytes=VMEM_LIMIT)


def _in_proj_prompt_kernel(x_ref, g_ref, w_ref, q_ref, kt_ref, vt_ref, kb_ref, vtb_ref, u_ref, km_ref):
    a = ATTN_WIDTH
    hn = _rms(x_ref[...], g_ref[...]).astype(_BF)
    q_ref[...] = _dot(hn, w_ref[:, 0:a])
    k = _dot(hn, w_ref[:, a:2 * a])
    v = _dot(hn, w_ref[:, 2 * a:3 * a])
    kt_ref[...] = k.T
    vt = v.T
    vt_ref[...] = vt
    kb_ref[...] = k.astype(_BF)
    u_ref[...] = _dot(hn, w_ref[:, 3 * a:])
    for r in range(k.shape[0] // MOBA_BLOCK):
        rows = slice(r * MOBA_BLOCK, (r + 1) * MOBA_BLOCK)
        vtb_ref[r] = vt[:, rows].astype(_BF)
        km_ref[r] = jnp.mean(k[rows, :], axis=0, keepdims=True)


def _in_proj_prompt(x, g, w, *, tm):
    b, t, d = x.shape
    a = ATTN_WIDTH
    su = w.shape[1] - 3 * a
    nt = t // tm
    nbt = tm // MOBA_BLOCK
    nbg = b // SEQS_PER_GROUP
    row = lambda bi, ti: (bi * nt + ti, 0)
    tok_minor = lambda bi, ti: (bi, 0, ti)
    return pl.pallas_call(
        _in_proj_prompt_kernel,
        grid=(b, nt),
        in_specs=[pl.BlockSpec((tm, d), row),
                  pl.BlockSpec((1, d), lambda bi, ti: (0, 0)),
                  pl.BlockSpec(w.shape, lambda bi, ti: (0, 0))],
        out_specs=[pl.BlockSpec((tm, a), row),
                   pl.BlockSpec((None, a, tm), tok_minor),
                   pl.BlockSpec((None, a, tm), tok_minor),
                   pl.BlockSpec((tm, a), row),
                   pl.BlockSpec((None, nbt, a, MOBA_BLOCK), lambda bi, ti: (bi, ti, 0, 0)),
                   pl.BlockSpec((None, tm, su), lambda bi, ti: (bi // SEQS_PER_GROUP, ti, bi % SEQS_PER_GROUP)),
                   pl.BlockSpec((nbt, 1, a), lambda bi, ti: (bi * nt + ti, 0, 0))],
        out_shape=[jax.ShapeDtypeStruct((b * t, a), _F32),
                   jax.ShapeDtypeStruct((b, a, t), _F32),
                   jax.ShapeDtypeStruct((b, a, t), _F32),
                   jax.ShapeDtypeStruct((b * t, a), _BF),
                   jax.ShapeDtypeStruct((b, t // MOBA_BLOCK, a, MOBA_BLOCK), _BF),
                   jax.ShapeDtypeStruct((nbg, t, SEQS_PER_GROUP * su), _F32),
                   jax.ShapeDtypeStruct((b * t // MOBA_BLOCK, 1, a), _F32)],
        compiler_params=_params(("parallel", "parallel")),
        name="in_proj_prompt",
    )(x.reshape(b * t, d), g, w)


def _norm_proj_kernel(x_ref, g_ref, w_ref, *o_refs):
    hn = _rms(x_ref[...], g_ref[...]).astype(_BF)
    c0 = 0
    for o_ref in o_refs:
        c1 = c0 + o_ref.shape[1]
        o_ref[...] = _dot(hn, w_ref[:, c0:c1])
        c0 = c1


def _norm_proj(x, g, w, widths, *, tm, name):
    n, d = x.shape
    row = lambda i: (i, 0)
    return pl.pallas_call(
        _norm_proj_kernel,
        grid=(n // tm,),
        in_specs=[pl.BlockSpec((tm, d), row),
                  pl.BlockSpec((1, d), lambda i: (0, 0)),
                  pl.BlockSpec(w.shape, lambda i: (0, 0))],
        out_specs=[pl.BlockSpec((tm, c), row) for c in widths],
        out_shape=[jax.ShapeDtypeStruct((n, c), _F32) for c in widths],
        compiler_params=_params(("parallel",)),
        name=name,
    )(x, g, w)


def _moba_prompt_kernel(q_ref, k_ref, vt_ref, km_ref, o_ref,
                        sel_ref, bias_ref, qx_ref, sa_ref, sb_ref, m_ref, l_ref, acc_ref, *, nb, nbp):
    blk = MOBA_BLOCK
    nh = ATTN_HEADS
    i = pl.program_id(1)
    q = q_ref[...]
    slopes2 = [s * _LOG2E for s in _slopes()]
    kk = lax.broadcasted_iota(jnp.int32, (blk, blk), 0)
    qq = lax.broadcasted_iota(jnp.int32, (blk, blk), 1)
    dkq = (kk - qq).astype(_F32)

    @pl.when(i == 0)
    def _fill_bias():
        for h in range(nh):
            bias_ref[h] = slopes2[h] * dkq

    gt = _dot_nt_3pass(km_ref[...], q)
    rows = lax.broadcasted_iota(jnp.int32, (nbp, blk), 0)
    fully_past = rows < i
    g = [jnp.where(fully_past, gt[h * nbp:(h + 1) * nbp, :], _NEG) for h in range(nh)]
    rank = [jnp.zeros((nbp, blk), _F32) for _ in range(nh)]
    for m in range(nb):
        tie = jnp.where(rows > m, 1.0, 0.0)
        for h in range(nh):
            rank[h] = _rank_update(rank[h], g[h][m:m + 1, :], g[h], tie)
    for h in range(nh):
        sel_ref[h * nbp:(h + 1) * nbp, :] = jnp.where(fully_past & (rank[h] < MOBA_TOPK), 1.0, 0.0)

    qs = (q * (HEAD_DIM ** -0.5 * _LOG2E)).astype(_BF)
    lane = lax.broadcasted_iota(jnp.int32, (blk, LANES), 1)
    zero = jnp.zeros((), _BF)
    for p in range(nh // 2):
        qp = qs[:, p * LANES:(p + 1) * LANES]
        qx_ref[p, 0:blk, :] = jnp.where(lane < HEAD_DIM, qp, zero)
        qx_ref[p, blk:2 * blk, :] = jnp.where(lane >= HEAD_DIM, qp, zero)

    causal_neg = jnp.where(dkq <= 0.0, 0.0, _NEG)
    ones_rows = jnp.ones((8, blk), _BF)

    def scores(j, s_out, p):
        start = pl.multiple_of(j * blk, blk)
        kb = k_ref[pl.ds(start, blk), p * LANES:(p + 1) * LANES]
        s_out[p] = _dot_nt(kb, qx_ref[p])

    def step(j_next, s_next, j, s_cur, own):
        coff = None if own else ((i - j) * blk).astype(_F32)
        for p in range(nh // 2):
            if j_next is not None:
                scores(j_next, s_next, p)
            vb = vt_ref[j, p * LANES:(p + 1) * LANES, :]
            ptbs = []
            alphas = []
            for x in range(2):
                h = 2 * p + x
                s = s_cur[p, :, x * blk:(x + 1) * blk] + bias_ref[h]
                if own:
                    s = s + causal_neg
                    m_new = jnp.max(s, axis=0, keepdims=True)
                    t = m_new
                    alphas.append(None)
                else:
                    c = slopes2[h] * coff
                    picked = sel_ref[pl.ds(h * nbp + j, 1), :] > 0.5
                    m_old = m_ref[h]
                    m_new = jnp.where(picked, jnp.maximum(m_old, jnp.max(s, axis=0, keepdims=True) - c), m_old)
                    t = jnp.where(picked, m_new + c, _BIG)
                    alphas.append(jnp.exp2(m_old - m_new))
                m_ref[h] = m_new
                ptbs.append(jnp.exp2(s - t).astype(_BF))
            for x in range(2):
                h = 2 * p + x
                arow = slice(h * HEAD_DIM, (h + 1) * HEAD_DIM)
                psum = _dot(ones_rows, ptbs[x])[0:1, :]
                pv = _dot(vb[x * HEAD_DIM:(x + 1) * HEAD_DIM, :], ptbs[x])
                if own:
                    l_ref[h] = psum
                    acc_ref[arow, :] = pv
                else:
                    l_ref[h] = alphas[x] * l_ref[h] + psum
                    acc_ref[arow, :] = acc_ref[arow, :] * alphas[x] + pv

    for p in range(nh // 2):
        scores(i, sa_ref, p)
    step(0, sb_ref, i, sa_ref, True)
    last = jnp.maximum(i - 1, 0)

    def pair(mi, carry):
        b0 = 2 * mi
        step(b0 + 1, sa_ref, b0, sb_ref, False)
        step(jnp.minimum(b0 + 2, last), sb_ref, b0 + 1, sa_ref, False)
        return carry

    lax.fori_loop(0, i // 2, pair, 0)

    @pl.when(i % 2 == 1)
    def _odd_tail():
        step(None, None, i - 1, sb_ref, False)

    for h in range(nh):
        arow = slice(h * HEAD_DIM, (h + 1) * HEAD_DIM)
        o_ref[arow, :] = acc_ref[arow, :] * (1.0 / l_ref[h])


def _moba_prompt(q, kb, vt, kmblk):
    b, t, a = kb.shape
    blk = MOBA_BLOCK
    nb = t // blk
    nbp = kmblk.shape[1] // ATTN_HEADS
    return pl.pallas_call(
        functools.partial(_moba_prompt_kernel, nb=nb, nbp=nbp),
        grid=(b, nb),
        in_specs=[pl.BlockSpec((blk, a), lambda bi, i: (bi * nb + i, 0)),
                  pl.BlockSpec((None, t, a), lambda bi, i: (bi, 0, 0)),
                  pl.BlockSpec((None, nb, a, blk), lambda bi, i: (bi, 0, 0, 0)),
                  pl.BlockSpec((None,) + kmblk.shape[1:], lambda bi, i: (bi, 0, 0))],
        out_specs=pl.BlockSpec((None, a, blk), lambda bi, i: (bi, 0, i)),
        out_shape=jax.ShapeDtypeStruct((b, a, t), _F32),
        scratch_shapes=[pltpu.VMEM((ATTN_HEADS * nbp, blk), _F32),
                        pltpu.VMEM((ATTN_HEADS, blk, blk), _F32),
                        pltpu.VMEM((ATTN_HEADS // 2, 2 * blk, LANES), _BF),
                        pltpu.VMEM((ATTN_HEADS // 2, blk, 2 * blk), _F32),
                        pltpu.VMEM((ATTN_HEADS // 2, blk, 2 * blk), _F32),
                        pltpu.VMEM((ATTN_HEADS, 1, blk), _F32),
                        pltpu.VMEM((ATTN_HEADS, 1, blk), _F32),
                        pltpu.VMEM((a, blk), _F32)],
        compiler_params=_params(("parallel", "arbitrary")),
        name="moba_prompt",
    )(q, kb, vt, kmblk)


def _moba_sample_kernel(pt_ref, wq_ref, kn_ref, vn_ref, slope_ref, qpos_ref, *rest,
                        pg, n_pages, page, past, dec):
    del pt_ref
    k_refs = rest[:pg]
    v_refs = rest[pg:2 * pg]
    o_ref, s_ref, gate_ref, acc_ref, l_ref = rest[2 * pg:]
    blk = MOBA_BLOCK
    ppb = blk // page
    nb = n_pages // ppb
    n_groups = n_pages // pg
    step = pl.program_id(1)
    wq = wq_ref[...]
    nq = wq.shape[0]
    wqb = (wq * (HEAD_DIM ** -0.5 * _LOG2E)).astype(_BF)
    blk_lane = lax.broadcasted_iota(jnp.int32, (nq, LANES), 1)

    @pl.when(step == 0)
    def _init():
        gate_ref[...] = jnp.zeros(gate_ref.shape, _F32)

    @pl.when(step < n_groups)
    def _k_phase():
        for c in range(pg // ppb):
            gs = jnp.zeros((nq, page), _F32)
            for r in range(c * ppb, (c + 1) * ppb):
                st = _dot(wqb, k_refs[r][...].astype(_BF))
                s_ref[step * pg + r] = st
                gs = gs + st
            n = step * (pg // ppb) + c
            gsum = jnp.sum(gs, axis=1, keepdims=True)
            gate_ref[...] = jnp.where(blk_lane == n, gsum, gate_ref[...])

    @pl.when(step == n_groups - 1)
    def _softmax():
        gate = gate_ref[...]
        rank = jnp.zeros(gate.shape, _F32)
        for m in range(nb):
            tie = jnp.where(blk_lane > m, 1.0, 0.0)
            rank = _rank_update(rank, gate[:, m:m + 1], gate, tie)
        drop = jnp.where(rank < MOBA_TOPK, 0.0, _NEG)
        lane = lax.broadcasted_iota(jnp.int32, (nq, page), 1).astype(_F32)
        slope2 = jnp.broadcast_to(slope_ref[...] * _LOG2E, (nq, page))
        qpos = jnp.broadcast_to(qpos_ref[...], (nq, page))
        base = slope2 * (lane - qpos)

        mm = jnp.full((nq, page), _NEG, _F32)
        for p in range(n_pages):
            n = p // ppb
            z = s_ref[p] + (base + slope2 * float(p * page)) + drop[:, n:n + 1]
            s_ref[p] = z
            mm = jnp.maximum(mm, z)
        so = _dot_nt(wqb, kn_ref[...]) + (base + slope2 * float(past))
        so = jnp.where(lane + float(past) <= qpos, so, _NEG)
        m = jnp.max(jnp.maximum(mm, so), axis=1, keepdims=True)
        ls = jnp.exp2(so - m)
        acc_ref[...] = _dot(ls.astype(_BF), vn_ref[...])
        for p in range(n_pages):
            pp = jnp.exp2(s_ref[p] - m)
            s_ref[p] = pp
            ls = ls + pp
        l_ref[...] = jnp.sum(ls, axis=1, keepdims=True)

    @pl.when(step >= n_groups)
    def _v_phase():
        acc = acc_ref[...]
        for r in range(pg):
            pp = s_ref[(step - n_groups) * pg + r]
            acc = acc + _dot_nt(pp.astype(_BF), v_refs[r][...].astype(_BF))
        acc_ref[...] = acc

    @pl.when(step == 2 * n_groups - 1)
    def _finish():
        o = acc_ref[...] * (1.0 / l_ref[...])
        lane_a = lax.broadcasted_iota(jnp.int32, (dec, o.shape[1]), 1)
        out = jnp.zeros((dec, o.shape[1]), _F32)
        for h in range(ATTN_HEADS):
            in_head = (lane_a >= h * HEAD_DIM) & (lane_a < (h + 1) * HEAD_DIM)
            out = out + jnp.where(in_head, o[h * dec:(h + 1) * dec, :], 0.0)
        o_ref[...] = out


def _moba_sample(page_table, wq, kn, vn, slope_rows, qpos_rows, cache_kt, cache_vt, *, pg, past):
    nseq, n_pages = page_table.shape
    _, a, page = cache_kt.shape
    nq = wq.shape[1]
    dec = nq // ATTN_HEADS
    n_groups = n_pages // pg
    assert n_pages * page // MOBA_BLOCK <= LANES
    seq = lambda bi, s, pt: (bi, 0, 0)
    const = lambda bi, s, pt: (0, 0)

    def k_map(r):
        return lambda bi, s, pt: (pt[bi, jnp.minimum(s, n_groups - 1) * pg + r], 0, 0)

    def v_map(r):
        def index(bi, s, pt):
            in_v = s >= n_groups
            seq_i = jnp.where(in_v, bi, jnp.maximum(bi - 1, 0))
            grp = jnp.where(in_v, s - n_groups, n_groups - 1)
            return (pt[seq_i, grp * pg + r], 0, 0)
        return index

    in_specs = [pl.BlockSpec((None, nq, a), seq),
                pl.BlockSpec((None, page, a), seq),
                pl.BlockSpec((None, page, a), seq),
                pl.BlockSpec((nq, 1), const),
                pl.BlockSpec((nq, 1), const)]
    in_specs += [pl.BlockSpec((None, a, page), k_map(r)) for r in range(pg)]
    in_specs += [pl.BlockSpec((None, a, page), v_map(r)) for r in range(pg)]
    return pl.pallas_call(
        functools.partial(_moba_sample_kernel, pg=pg, n_pages=n_pages, page=page, past=past, dec=dec),
        grid_spec=pltpu.PrefetchScalarGridSpec(
            num_scalar_prefetch=1,
            grid=(nseq, 2 * n_groups),
            in_specs=in_specs,
            out_specs=pl.BlockSpec((None, dec, a), seq),
            scratch_shapes=[pltpu.VMEM((n_pages, nq, page), _F32),
                            pltpu.VMEM((nq, LANES), _F32),
                            pltpu.VMEM((nq, a), _F32),
                            pltpu.VMEM((nq, 1), _F32)]),
        out_shape=jax.ShapeDtypeStruct((nseq, dec, a), _F32),
        compiler_params=_params(("parallel", "arbitrary")),
        name="moba_sample",
    )(page_table, wq, kn, vn, slope_rows, qpos_rows, *([cache_kt] * pg), *([cache_vt] * pg))


def _s5_disc_kernel(are_ref, aim_ref, ldt_ref, bre_ref, bim_ref, lbr_ref, lbi_ref, bbr_ref, bbi_ref):
    are = are_ref[...]
    aim = aim_ref[...]
    dt = jnp.exp(ldt_ref[...])
    mag = jnp.exp(are * dt)
    ang = aim * dt
    lb_re = mag * jnp.cos(ang)
    lb_im = mag * jnp.sin(ang)
    e_re = lb_re - 1.0
    e_im = lb_im
    den = are * are + aim * aim
    f_re = (e_re * are + e_im * aim) / den
    f_im = (e_im * are - e_re * aim) / den
    br = bre_ref[...]
    bi = bim_ref[...]
    lbr_ref[...] = lb_re
    lbi_ref[...] = lb_im
    bbr_ref[...] = f_re * br - f_im * bi
    bbi_ref[...] = f_re * bi + f_im * br


def _s5_disc(a_re, a_im, log_dt, b_re, b_im):
    g, p, c = b_re.shape
    rep = lambda z: jnp.repeat(z, c, axis=0)
    are, aim = rep(a_re), rep(a_im)
    ldt = rep(log_dt.reshape(g, 1))
    b2 = lambda z: jnp.transpose(z, (0, 2, 1)).reshape(g * c, p)
    shp = jax.ShapeDtypeStruct((g * c, p), _F32)
    lbr, lbi, bbr, bbi = pl.pallas_call(
        _s5_disc_kernel, out_shape=[shp] * 4, name="s5_disc",
    )(are, aim, ldt, b2(b_re), b2(b_im))
    return lbr[::c], lbi[::c], bbr, bbi


def _s5_kernel(u_ref, pm_ref, pmt_ref, h0r_ref, h0i_ref, lbr_ref, lbi_ref, bbr_ref, bbi_ref, cr_ref, ci_ref, d_ref,
               wg_ref, bg_ref, y_ref, hr_out, hi_out, hre, him, st_r, st_i, *, tc, cw, seq_lanes):
    spg = SEQS_PER_GROUP
    t = pl.program_id(1)
    ns = hre.shape[1]
    w = wg_ref.shape[0]

    @pl.when(t == 0)
    def _init():
        st_r[...] = h0r_ref[...]
        st_i[...] = h0i_ref[...]

    if seq_lanes:
        u_st = jnp.concatenate([u_ref[:, b * w:(b + 1) * w] for b in range(spg)], axis=0)
        u_hi, u_lo = _split_bf16(u_st)
        u = _dot(pm_ref[...], u_hi) + _dot(pm_ref[...], u_lo)
    else:
        u = u_ref[...]
    ub = u.astype(_BF)
    ratio = ns // w
    for n in range(ns // MXU_N):
        out = slice(n * MXU_N, (n + 1) * MXU_N)
        lo = (n * MXU_N // ratio) // LANES * LANES
        src = slice(lo, lo + LANES)
        hre[:, out] = _dot(ub[:, src], bbr_ref[src, out])
        him[:, out] = _dot(ub[:, src], bbi_ref[src, out])

    for c in range(ns // cw):
        cs = slice(c * cw, (c + 1) * cw)
        lr = jnp.broadcast_to(lbr_ref[:, cs], (spg, cw))
        li = jnp.broadcast_to(lbi_ref[:, cs], (spg, cw))

        def step(tt, carry):
            sr, si = carry
            r0 = pl.multiple_of(tt * spg, spg)
            nr = lr * sr - li * si + hre[pl.ds(r0, spg), cs]
            ni = lr * si + li * sr + him[pl.ds(r0, spg), cs]
            hre[pl.ds(r0, spg), cs] = nr
            him[pl.ds(r0, spg), cs] = ni
            return nr, ni

        sr, si = lax.fori_loop(0, tc, step, (st_r[:, cs], st_i[:, cs]), unroll=True)
        st_r[:, cs] = sr
        st_i[:, cs] = si

    parts = []
    for m in range(w // MXU_N):
        out = slice(m * MXU_N, (m + 1) * MXU_N)
        src = slice(m * MXU_N * ratio, (m + 1) * MXU_N * ratio)
        parts.append(_dot(hre[:, src].astype(_BF), cr_ref[src, out])
                     - _dot(him[:, src].astype(_BF), ci_ref[src, out]))
    y = jnp.concatenate(parts, axis=1) + d_ref[...] * u
    z = _dot(y.astype(_BF), wg_ref[...]) + bg_ref[...]
    y = y * (1.0 / (1.0 + jnp.exp(-z)))
    if seq_lanes:
        y_st = _dot(pmt_ref[...], y.astype(_BF)).astype(y_ref.dtype)
        for b in range(spg):
            y_ref[:, b * w:(b + 1) * w] = y_st[b * tc:(b + 1) * tc, :]
    else:
        y_ref[...] = y.astype(y_ref.dtype)
    hr_out[...] = st_r[...]
    hi_out[...] = st_i[...]


def _s5(u, h0r, h0i, lbr, lbi, bbr, bbi, cr, ci, d, wg, bg, *, tc, seq_lanes):
    spg = SEQS_PER_GROUP
    w = wg.shape[0]
    nbg = u.shape[0]
    t = u.shape[1] if seq_lanes else u.shape[1] // spg
    ns = lbr.shape[1]
    ratio = ns // w
    assert ns % MXU_N == 0 and w % MXU_N == 0 and LANES % (MXU_N // ratio) == 0
    const = lambda gi, ti: (0, 0)
    grp = lambda gi, ti: (gi, 0, 0)
    io_block = (None, tc, spg * w) if seq_lanes else (None, tc * spg, w)
    r = tc * spg
    dst = jnp.arange(r)
    pm = (dst[:, None] % spg * tc + dst[:, None] // spg == dst[None, :]).astype(_BF)
    return pl.pallas_call(
        functools.partial(_s5_kernel, tc=tc, cw=min(ns, 8 * LANES), seq_lanes=seq_lanes),
        grid=(nbg, t // tc),
        in_specs=[pl.BlockSpec(io_block, lambda gi, ti: (gi, ti, 0)),
                  pl.BlockSpec((r, r), const), pl.BlockSpec((r, r), const),
                  pl.BlockSpec((None, spg, ns), grp),
                  pl.BlockSpec((None, spg, ns), grp),
                  pl.BlockSpec((1, ns), const), pl.BlockSpec((1, ns), const),
                  pl.BlockSpec((w, ns), const), pl.BlockSpec((w, ns), const),
                  pl.BlockSpec((ns, w), const), pl.BlockSpec((ns, w), const),
                  pl.BlockSpec((1, w), const),
                  pl.BlockSpec((w, w), const), pl.BlockSpec((1, w), const)],
        out_specs=[pl.BlockSpec(io_block, lambda gi, ti: (gi, ti, 0)),
                   pl.BlockSpec((None, spg, ns), grp),
                   pl.BlockSpec((None, spg, ns), grp)],
        out_shape=[jax.ShapeDtypeStruct(u.shape, _BF if seq_lanes else _F32),
                   jax.ShapeDtypeStruct((nbg, spg, ns), _F32),
                   jax.ShapeDtypeStruct((nbg, spg, ns), _F32)],
        scratch_shapes=[pltpu.VMEM((r, ns), _F32), pltpu.VMEM((r, ns), _F32),
                        pltpu.VMEM((spg, ns), _F32), pltpu.VMEM((spg, ns), _F32)],
        compiler_params=_params(("parallel", "arbitrary")),
        name="s5",
    )(u, pm, pm.T, h0r, h0i, lbr, lbi, bbr, bbi, cr, ci, d, wg, bg)


def _block_diag(blocks):
    g, r, c = blocks.shape
    eye = jnp.eye(g, dtype=blocks.dtype)
    return (blocks[:, :, None, :] * eye[:, None, :, None]).reshape(g * r, g * c)


def _mix_kernel(x_ref, a_ref, s_ref, woa_ref, wos_ref, gc_ref, wcq_ref, x1_ref, cq_ref):
    x1 = (x_ref[...] + _dot(a_ref[...].astype(_BF), woa_ref[...])
          + _dot(s_ref[...].astype(_BF), wos_ref[...]))
    x1_ref[...] = x1
    cq_ref[...] = _dot(_rms(x1, gc_ref[...]).astype(_BF), wcq_ref[...])


def _mix(x, attn, ssm, ssm_map, woa, wos, gc, wcq, *, nb, nt, tm):
    n, d = x.shape
    row = lambda bi, ti: (bi * nt + ti, 0)
    const = lambda bi, ti: (0, 0)
    mw = wcq.shape[1]
    sw = wos.shape[0]
    return pl.pallas_call(
        _mix_kernel,
        grid=(nb, nt),
        in_specs=[pl.BlockSpec((tm, d), row),
                  pl.BlockSpec((tm, attn.shape[1]), row),
                  pl.BlockSpec((tm, sw), ssm_map),
                  pl.BlockSpec(woa.shape, const), pl.BlockSpec(wos.shape, const),
                  pl.BlockSpec((1, d), const), pl.BlockSpec(wcq.shape, const)],
        out_specs=[pl.BlockSpec((tm, d), row), pl.BlockSpec((tm, mw), row)],
        out_shape=[jax.ShapeDtypeStruct((n, d), _F32), jax.ShapeDtypeStruct((n, mw), _F32)],
        compiler_params=_params(("parallel", "parallel")),
        name="mix",
    )(x, attn, ssm, woa, wos, gc, wcq)


def _mem_kv_kernel(x_ref, g_ref, w_ref, mk_ref, mv_ref, mkb_ref, mvb_ref):
    hn = _rms(x_ref[...], g_ref[...]).astype(_BF)
    mw = MEM_HEADS * MEM_HEAD_DIM
    k = _dot(hn, w_ref[:, :mw])
    v = _dot(hn, w_ref[:, mw:])
    mkb_ref[...] = k.astype(_BF)
    mvb_ref[...] = v.astype(_BF)
    for h in range(MEM_HEADS):
        cols = slice(h * MEM_HEAD_DIM, (h + 1) * MEM_HEAD_DIM)
        mk_ref[:, h, :] = k[:, cols]
        mv_ref[:, h, :] = v[:, cols]


def _mem_kv(mem, g, w, *, tm):
    n, d = mem.shape
    mw = MEM_HEADS * MEM_HEAD_DIM
    shp = jax.ShapeDtypeStruct((n, MEM_HEADS, MEM_HEAD_DIM), _F32)
    spec = pl.BlockSpec((tm, MEM_HEADS, MEM_HEAD_DIM), lambda i: (i, 0, 0))
    flat = pl.BlockSpec((tm, mw), lambda i: (i, 0))
    return pl.pallas_call(
        _mem_kv_kernel,
        grid=(n // tm,),
        in_specs=[pl.BlockSpec((tm, d), lambda i: (i, 0)),
                  pl.BlockSpec((1, d), lambda i: (0, 0)),
                  pl.BlockSpec(w.shape, lambda i: (0, 0))],
        out_specs=[spec, spec, flat, flat],
        out_shape=[shp, shp, jax.ShapeDtypeStruct((n, mw), _BF), jax.ShapeDtypeStruct((n, mw), _BF)],
        compiler_params=_params(("parallel",)),
        name="mem_kv",
    )(mem, g, w)


def _cross_attn_kernel(q_ref, mk_ref, mv_ref, o_ref, *, sb, rq):
    nrow = MEM_HEADS * rq
    nmem = mk_ref.shape[1]
    row_head = lax.broadcasted_iota(jnp.int32, (nrow, nmem), 0) // rq
    col_head = lax.broadcasted_iota(jnp.int32, (nrow, nmem), 1) % MEM_HEADS
    same_head = row_head == col_head
    for s in range(sb):
        q = q_ref[s * rq:(s + 1) * rq, :]
        wq = jnp.concatenate([q[:, h * MEM_HEAD_DIM:(h + 1) * MEM_HEAD_DIM] for h in range(MEM_HEADS)], axis=0)
        sc = _dot_nt(wq.astype(_BF), mk_ref[s].astype(_BF)) * (MEM_HEAD_DIM ** -0.5)
        sc = jnp.where(same_head, sc, _NEG)
        m = jnp.max(sc, axis=-1, keepdims=True)
        e = jnp.exp(sc - m)
        p = e * (1.0 / jnp.sum(e, axis=-1, keepdims=True))
        o = _dot(p.astype(_BF), mv_ref[s].astype(_BF))
        o_ref[s * rq:(s + 1) * rq, :] = jnp.concatenate(
            [o[h * rq:(h + 1) * rq, :] for h in range(MEM_HEADS)], axis=1)


def _cross_attn(cq, mk, mv, *, sb, rq):
    n, mw = cq.shape
    nseq, nmem, dh = mk.shape
    mem = lambda si: (si, 0, 0)
    return pl.pallas_call(
        functools.partial(_cross_attn_kernel, sb=sb, rq=rq),
        grid=(nseq // sb,),
        in_specs=[pl.BlockSpec((sb * rq, mw), lambda si: (si, 0)),
                  pl.BlockSpec((sb, nmem, dh), mem),
                  pl.BlockSpec((sb, nmem, dh), mem)],
        out_specs=pl.BlockSpec((sb * rq, mw), lambda si: (si, 0)),
        out_shape=jax.ShapeDtypeStruct((n, mw), _F32),
        compiler_params=_params(("parallel",)),
        name="cross_attn",
    )(cq, mk, mv)


def _mlp_tail(x2, gm_ref, w1_ref, w2_ref, gf_ref, fc):
    hn = _rms(x2, gm_ref[...]).astype(_BF)
    acc = x2
    for c in range(w1_ref.shape[1] // fc):
        cols = slice(c * fc, (c + 1) * fc)
        hid = jnp.square(jnp.maximum(_dot(hn, w1_ref[:, cols]), 0.0)).astype(_BF)
        acc = acc + _dot(hid, w2_ref[cols, :])
    return _rms(acc, gf_ref[...])


def _mlp_kernel(x1_ref, o_ref, wco_ref, gm_ref, w1_ref, w2_ref, gf_ref, y_ref, *, fc):
    x2 = x1_ref[...] + _dot(o_ref[...].astype(_BF), wco_ref[...])
    y_ref[...] = _mlp_tail(x2, gm_ref, w1_ref, w2_ref, gf_ref, fc)


def _post_kernel(x_ref, a_ref, s_ref, mk_ref, mv_ref, woa_ref, wos_ref, gc_ref, wcq_ref, wco_ref,
                 gm_ref, w1_ref, w2_ref, gf_ref, y_ref, *, fc):
    x1 = (x_ref[...] + _dot(a_ref[...].T.astype(_BF), woa_ref[...])
          + _dot(s_ref[...].astype(_BF), wos_ref[...]))
    cq = _dot(_rms(x1, gc_ref[...]).astype(_BF), wcq_ref[...]).astype(_BF)
    heads = []
    for h in range(MEM_HEADS):
        cols = slice(h * MEM_HEAD_DIM, (h + 1) * MEM_HEAD_DIM)
        sc = _dot_nt(cq[:, cols], mk_ref[:, cols]) * (MEM_HEAD_DIM ** -0.5)
        m = jnp.max(sc, axis=-1, keepdims=True)
        e = jnp.exp(sc - m)
        p = e * (1.0 / jnp.sum(e, axis=-1, keepdims=True))
        heads.append(_dot(p.astype(_BF), mv_ref[:, cols]))
    o = jnp.concatenate(heads, axis=1)
    x2 = x1 + _dot(o.astype(_BF), wco_ref[...])
    y_ref[...] = _mlp_tail(x2, gm_ref, w1_ref, w2_ref, gf_ref, fc)


def _post(x, attn, ssm, ssm_map, mkb, mvb, woa, wos, gc, wcq, wco, gm, w1, w2, gf, *, nb, nt, tm):
    n, d = x.shape
    row = lambda bi, ti: (bi * nt + ti, 0)
    const = lambda bi, ti: (0, 0)
    mem = lambda bi, ti: (bi, 0, 0)
    resident = lambda z: pl.BlockSpec(z.shape, const, pipeline_mode=pl.Buffered(1))
    return pl.pallas_call(
        functools.partial(_post_kernel, fc=min(w1.shape[1], 1024)),
        grid=(nb, nt),
        in_specs=[pl.BlockSpec((tm, d), row),
                  pl.BlockSpec((None, attn.shape[1], tm), lambda bi, ti: (bi, 0, ti)),
                  pl.BlockSpec((tm, wos.shape[0]), ssm_map),
                  pl.BlockSpec((None,) + mkb.shape[1:], mem), pl.BlockSpec((None,) + mvb.shape[1:], mem),
                  resident(woa), resident(wos), resident(gc), resident(wcq), resident(wco),
                  resident(gm), resident(w1), resident(w2), resident(gf)],
        out_specs=pl.BlockSpec((tm, d), row),
        out_shape=jax.ShapeDtypeStruct((n, d), _F32),
        compiler_params=_params(("parallel", "parallel")),
        name="post",
    )(x, attn, ssm, mkb, mvb, woa, wos, gc, wcq, wco, gm, w1, w2, gf)


def _mlp(x1, o, wco, gm, w1, w2, gf, *, tm):
    n, d = x1.shape
    row = lambda i: (i, 0)
    const = lambda i: (0, 0)
    return pl.pallas_call(
        functools.partial(_mlp_kernel, fc=min(w1.shape[1], 1024)),
        grid=(n // tm,),
        in_specs=[pl.BlockSpec((tm, d), row), pl.BlockSpec((tm, o.shape[1]), row),
                  pl.BlockSpec(wco.shape, const), pl.BlockSpec((1, d), const),
                  pl.BlockSpec(w1.shape, const), pl.BlockSpec(w2.shape, const),
                  pl.BlockSpec((1, d), const)],
        out_specs=pl.BlockSpec((tm, d), row),
        out_shape=jax.ShapeDtypeStruct((n, d), _F32),
        compiler_params=_params(("parallel",)),
        name="mlp",
    )(x1, o, wco, gm, w1, w2, gf)


def kernel(x_prompt, x_sample, cache_k, cache_v, state_ssm_re, state_ssm_im, cache_mem_k, cache_mem_v,
           page_table, mem_prompt, g_mix, w_in, ssm_a_re, ssm_a_im, ssm_log_dt, ssm_b_re, ssm_b_im,
           ssm_c_re, ssm_c_im, ssm_d, w_glu, b_glu, w_out, g_mem, w_ck, w_cv, g_cross, w_cq, w_co,
           g_mlp, w_ff1, w_ff2, g_final):
    depth = w_in.shape[0]
    assert depth == 1, "single-layer trunk"
    b, t, d = x_prompt.shape
    nseq, dec, _ = x_sample.shape
    a = ATTN_WIDTH
    spg = SEQS_PER_GROUP
    blk = MOBA_BLOCK
    assert t % blk == 0 and b % spg == 0 and nseq % spg == 0 and dec == spg
    n_phys, page = cache_k.shape[1], cache_k.shape[2]
    n_pages = page_table.shape[1]
    past = n_pages * page
    assert past % blk == 0 and blk % page == 0 and dec <= page
    nb = t // blk
    nbp = -(-nb // 8) * 8
    groups, ns_p = ssm_a_re.shape[1], ssm_a_re.shape[2]
    ns = groups * ns_p
    sw = groups * SSM_GROUP
    mtok = mem_prompt.shape[1]
    row2 = lambda z: z.reshape(1, -1)
    hd = (ATTN_HEADS, HEAD_DIM)

    w_in_b = w_in[0].astype(_BF)
    wo_a = w_out[0, :a].astype(_BF)
    wo_s = w_out[0, a:].astype(_BF)
    w_cq_b, w_co_b = w_cq[0].astype(_BF), w_co[0].astype(_BF)
    w1_b, w2_b = w_ff1[0].astype(_BF), w_ff2[0].astype(_BF)
    w_glu_b = w_glu[0].astype(_BF)
    w_ckv_b = jnp.concatenate([w_ck[0], w_cv[0]], axis=1).astype(_BF)
    lbr, lbi, bbr, bbi = _s5_disc(ssm_a_re[0], ssm_a_im[0], ssm_log_dt[0], ssm_b_re[0], ssm_b_im[0])
    lbr, lbi = row2(lbr), row2(lbi)
    bb_r = _block_diag(bbr.reshape(groups, SSM_GROUP, ns_p)).astype(_BF)
    bb_i = _block_diag(bbi.reshape(groups, SSM_GROUP, ns_p)).astype(_BF)
    c_r = _block_diag(jnp.transpose(ssm_c_re[0], (0, 2, 1))).astype(_BF)
    c_i = _block_diag(jnp.transpose(ssm_c_im[0], (0, 2, 1))).astype(_BF)
    s5_w = (lbr, lbi, bb_r, bb_i, c_r, c_i, row2(ssm_d[0]), w_glu_b, row2(b_glu[0]))

    tm = 512 if t % 512 == 0 else blk
    q_p, kt_p, vt_p, kb_p, vtb_p, u_p, km_p = _in_proj_prompt(x_prompt, row2(g_mix[0]), w_in_b, tm=tm)
    km = km_p.reshape(b, nb, ATTN_HEADS, HEAD_DIM)
    km = jnp.pad(jnp.transpose(km, (0, 2, 1, 3)), ((0, 0), (0, 0), (0, nbp - nb), (0, 0)))
    kmblk = (km[:, :, :, None, :] * jnp.eye(ATTN_HEADS, dtype=_F32)[None, :, None, :, None]
             ).reshape(b, ATTN_HEADS * nbp, a)
    attn_p = _moba_prompt(q_p, kb_p.reshape(b, t, a), vtb_p, kmblk)

    zeros_state = jnp.zeros((b // spg, spg, ns), _F32)
    tc = 64 if t % 64 == 0 else t
    ssm_p, hr_p, hi_p = _s5(u_p, zeros_state, zeros_state, *s5_w, tc=tc, seq_lanes=True)

    nt = t // tm
    ssm_map_p = lambda bi, ti: (bi // spg * nt + ti, bi % spg)
    mk_p, mv_p, mkb_p, mvb_p = _mem_kv(mem_prompt.reshape(b * mtok, d), row2(g_mem[0]), w_ckv_b, tm=mtok)
    mem4 = (b, mtok, MEM_HEADS, MEM_HEAD_DIM)
    mw = MEM_HEADS * MEM_HEAD_DIM
    y_p = _post(x_prompt.reshape(b * t, d), attn_p, ssm_p.reshape(b // spg * t, spg * sw), ssm_map_p,
                mkb_p.reshape(b, mtok, mw), mvb_p.reshape(b, mtok, mw),
                wo_a, wo_s, row2(g_cross[0]), w_cq_b, w_co_b, row2(g_mlp[0]), w1_b, w2_b, row2(g_final),
                nb=b, nt=nt, tm=tm)

    n_s = nseq * dec
    tms = 256 if n_s % 256 == 0 else n_s
    q_s, k_s, v_s, u_s = _norm_proj(x_sample.reshape(n_s, d), row2(g_mix[0]), w_in_b, (a, a, a, sw),
                                    tm=tms, name="in_proj_sample")
    nq = ATTN_HEADS * dec
    head_of_row = jnp.arange(nq) // dec
    head_of_col = jnp.arange(a) // HEAD_DIM
    head_mask = (head_of_row[:, None] == head_of_col[None, :]).astype(_F32)
    wq = jnp.tile(q_s.reshape(nseq, dec, a), (1, ATTN_HEADS, 1)) * head_mask[None]
    pad_new = lambda z: jnp.pad(z.reshape(nseq, dec, a), ((0, 0), (0, page - dec), (0, 0))).astype(_BF)
    slope_rows = jnp.asarray(_slopes(), _F32)[head_of_row].reshape(nq, 1)
    qpos_rows = (past + jnp.arange(nq) % dec).astype(_F32).reshape(nq, 1)
    token_minor = lambda c: jnp.transpose(c[0], (0, 2, 3, 1)).reshape(n_phys, a, page)
    pg = 32 if n_pages % 32 == 0 else blk // page
    attn_s = _moba_sample(page_table, wq, pad_new(k_s), pad_new(v_s), slope_rows, qpos_rows,
                          token_minor(cache_k), token_minor(cache_v), pg=pg, past=past)

    to_groups = lambda z: jnp.transpose(z.reshape(nseq // spg, spg, dec, -1), (0, 2, 1, 3))
    u_g = to_groups(u_s).reshape(nseq // spg, dec * spg, sw)
    h0r = state_ssm_re[0].reshape(nseq // spg, spg, ns)
    h0i = state_ssm_im[0].reshape(nseq // spg, spg, ns)
    ssm_g, hr_s, hi_s = _s5(u_g, h0r, h0i, *s5_w, tc=dec, seq_lanes=False)
    ssm_s = jnp.transpose(ssm_g.reshape(nseq // spg, dec, spg, sw), (0, 2, 1, 3)).reshape(n_s, sw)

    nts = n_s // tms
    x1_s, cq_s = _mix(x_sample.reshape(n_s, d), attn_s.reshape(n_s, a), ssm_s, lambda bi, ti: (ti, 0),
                      wo_a, wo_s, row2(g_cross[0]), w_cq_b, nb=1, nt=nts, tm=tms)
    mem_rows = lambda c: c[0].reshape(nseq, mtok * MEM_HEADS, MEM_HEAD_DIM)
    o_s = _cross_attn(cq_s, mem_rows(cache_mem_k), mem_rows(cache_mem_v), sb=spg, rq=dec)
    y_s = _mlp(x1_s, o_s, w_co_b, row2(g_mlp[0]), w1_b, w2_b, row2(g_final), tm=tms)

    kv_out = lambda z: jnp.transpose(z.reshape((1, b) + hd + (t,)), (0, 1, 4, 2, 3))
    return (y_p.reshape(b, t, d), y_s.reshape(nseq, dec, d),
            kv_out(kt_p), kv_out(vt_p),
            hr_p.reshape(1, b, groups, ns_p), hi_p.reshape(1, b, groups, ns_p),
            mk_p.reshape((1,) + mem4), mv_p.reshape((1,) + mem4),
            k_s.reshape((1, nseq, dec) + hd), v_s.reshape((1, nseq, dec) + hd),
            hr_s.reshape(1, nseq, groups, ns_p), hi_s.reshape(1, nseq, groups, ns_p))
```
